```python
import jax, jax.numpy as jnp
from jax import lax
import numpy as np

D_MODEL = 4096
BATCH = 8
SEQ = 2048
DEPTH = 2

D_FF = 11008
EPS = 1e-6

GRID_W = 64

GLA_HEADS = 4
GLA_DK = 192
GLA_DV = 384
GLA_RANK = 16
GLA_TAU = 16.0
GLA_CHUNK = 64
GLA_K_WIDTH = GLA_HEADS * GLA_DK
GLA_WIDTH = GLA_HEADS * GLA_DV

GMLP_GROUPS = 8
GMLP_GROUP_DIM = 128
GMLP_CHUNK = 128
GMLP_WIDTH = GMLP_GROUPS * GMLP_GROUP_DIM

NA_HEADS = 12
NA_HEAD_DIM = 128
NA_WIN_ROWS = 8
NA_WIN_COLS = 16
NA_WIDTH = NA_HEADS * NA_HEAD_DIM

MIX_WIDTH = GLA_WIDTH + GMLP_WIDTH + NA_WIDTH
N_BRANCHES = 3

IN_SIZES = (
    GLA_K_WIDTH, GLA_K_WIDTH, GLA_WIDTH, GLA_WIDTH, GLA_RANK, GLA_RANK,
    GMLP_WIDTH, GMLP_WIDTH,
    NA_WIDTH, NA_WIDTH, NA_WIDTH,
)
IN_WIDTH = sum(IN_SIZES)

kernel_name = "hybrid_gla_gmlp_natten_macaron_encoder"


def rms_norm(x, g):
    xf = x.astype(jnp.float32)
    y = xf * lax.rsqrt(jnp.mean(xf * xf, axis=-1, keepdims=True) + EPS)
    return (y * g.astype(jnp.float32)).astype(x.dtype)


def swiglu_ffn(h, w_in, w_out):
    a, b = jnp.split(h @ w_in, 2, axis=-1)
    return (jax.nn.silu(a) * b) @ w_out


def gla_chunked(q, k, v, log_a, strict):
    B, L, H, K = q.shape
    V = v.shape[-1]
    C = GLA_CHUNK
    N = L // C
    q = q.reshape(B, N, C, H, K)
    k = k.reshape(B, N, C, H, K)
    log_a = log_a.reshape(B, N, C, H, K)
    v = v.reshape(B, N, C, H, V)
    b = jnp.cumsum(log_a, axis=2)
    b_last = b[:, :, -1:]
    b_mid = b[:, :, C // 2 - 1:C // 2]
    q_intra = q * jnp.exp(b - b_mid)
    k_intra = k * jnp.exp(b_mid - b)
    scores = jnp.einsum('bnihk,bnjhk->bnhij', q_intra, k_intra)
    pos = jnp.arange(C)
    mask = (pos[None, :] < pos[:, None]) if strict else (pos[None, :] <= pos[:, None])
    scores = jnp.where(mask, scores, 0.0)
    o_intra = jnp.einsum('bnhij,bnjhv->bnihv', scores, v)
    q_inter = q * jnp.exp(b)
    k_state = k * jnp.exp(b_last - b)
    decay = jnp.exp(b_last[:, :, 0])

    def step(S, xs):
        qc, kc, vc, dc = xs
        o = jnp.einsum('bihk,bhkv->bihv', qc, S)
        S = dc[..., None] * S + jnp.einsum('bjhk,bjhv->bhkv', kc, vc)
        return S, o

    S0 = jnp.zeros((B, H, K, V), q.dtype)
    xs = (jnp.moveaxis(q_inter, 1, 0), jnp.moveaxis(k_state, 1, 0),
          jnp.moveaxis(v, 1, 0), jnp.moveaxis(decay, 1, 0))
    _, o_inter = lax.scan(step, S0, xs)
    o = o_intra + jnp.moveaxis(o_inter, 0, 1)
    return o.reshape(B, L, H, V)


def gla_mixer(q, k, v, g, a_f, a_b, w_a2_f, b_f, w_a2_b, b_b, out_norm):
    B, L, _ = q.shape
    dt = q.dtype
    f32 = jnp.float32
    q = (q.astype(f32) * GLA_DK ** -0.5).reshape(B, L, GLA_HEADS, GLA_DK)
    k = k.astype(f32).reshape(B, L, GLA_HEADS, GLA_DK)
    v = v.astype(f32).reshape(B, L, GLA_HEADS, GLA_DV)

    def log_decay(a, w2, bias):
        pre = (a @ w2 + bias).astype(f32)
        return (jax.nn.log_sigmoid(pre) / GLA_TAU).reshape(B, L, GLA_HEADS, GLA_DK)

    la_f = log_decay(a_f, w_a2_f, b_f)
    la_b = log_decay(a_b, w_a2_b, b_b)
    o_f = gla_chunked(q, k, v, la_f, strict=False)
    o_b = jnp.flip(gla_chunked(jnp.flip(q, 1), jnp.flip(k, 1), jnp.flip(v, 1), jnp.flip(la_b, 1), strict=True), 1)
    o = o_f + o_b
    o = o * lax.rsqrt(jnp.mean(o * o, axis=-1, keepdims=True) + EPS) * out_norm.astype(f32)
    o = o.reshape(B, L, GLA_WIDTH) * jax.nn.silu(g.astype(f32))
    return o.astype(dt)


def gmlp_mixer(u, v, ln_g, ln_b, w_s, b_s):
    B, L, W = u.shape
    dt = u.dtype
    u = jax.nn.gelu(u)
    vf = jax.nn.gelu(v).astype(jnp.float32)
    mu = jnp.mean(vf, axis=-1, keepdims=True)
    var = jnp.mean(jnp.square(vf - mu), axis=-1, keepdims=True)
    v = ((vf - mu) * lax.rsqrt(var + EPS) * ln_g.astype(jnp.float32) + ln_b.astype(jnp.float32)).astype(dt)
    N = L // GMLP_CHUNK
    v = v.reshape(B, N, GMLP_CHUNK, GMLP_GROUPS, GMLP_GROUP_DIM)
    s = jnp.einsum('gts,bnsgc->bntgc', w_s, v) + b_s.T[:, :, None]
    return u * s.reshape(B, L, W)


def na_mixer(q, k, v, rpb):
    B, L, _ = q.shape
    rows = L // GRID_W
    wr = min(NA_WIN_ROWS, rows)
    shp = (B, rows, GRID_W, NA_HEADS, NA_HEAD_DIM)
    q = (q * NA_HEAD_DIM ** -0.5).reshape(shp)
    k = k.reshape(shp)
    v = v.reshape(shp)
    cols = np.arange(GRID_W)
    c0 = np.clip(cols - NA_WIN_COLS // 2, 0, GRID_W - NA_WIN_COLS)
    col_mask = (cols[None, :] >= c0[:, None]) & (cols[None, :] < c0[:, None] + NA_WIN_COLS)
    dx_idx = np.clip(cols[None, :] - cols[:, None] + NA_WIN_COLS - 1, 0, 2 * NA_WIN_COLS - 2)
    rpb_cols = rpb[:, :, dx_idx]

    def row_block(r):
        r0 = jnp.clip(r - wr // 2, 0, rows - wr)
        q_row = lax.dynamic_index_in_dim(q, r, axis=1, keepdims=False)
        k_blk = lax.dynamic_slice_in_dim(k, r0, wr, axis=1)
        v_blk = lax.dynamic_slice_in_dim(v, r0, wr, axis=1)
        dy_idx = r0 + jnp.arange(wr) - r + NA_WIN_ROWS - 1
        bias = jnp.take(rpb_cols, dy_idx, axis=1)
        s = jnp.einsum('bqhd,bijhd->bhqij', q_row, k_blk).astype(jnp.float32)
        s = s + jnp.transpose(bias, (0, 2, 1, 3))[None].astype(jnp.float32)
        s = jnp.where(col_mask[None, None, :, None, :], s, -jnp.inf)
        p = jax.nn.softmax(s.reshape(B, NA_HEADS, GRID_W, wr * GRID_W), axis=-1)
        p = p.reshape(s.shape).astype(v.dtype)
        return jnp.einsum('bhqij,bijhd->bqhd', p, v_blk)

    o = lax.map(row_block, jnp.arange(rows))
    return jnp.moveaxis(o, 0, 1).reshape(B, L, NA_WIDTH)


def hybrid_mixer(h, w_in, w_gate, gla_w_a2_fwd, gla_b_fwd, gla_w_a2_bwd, gla_b_bwd, gla_out_norm,
                 gmlp_ln_g, gmlp_ln_b, gmlp_w_s, gmlp_b_s, na_rpb, w_branch, w_out):
    splits = np.cumsum(IN_SIZES)[:-1].tolist()
    (gq, gk, gv, gg, ga_f, ga_b, mu, mv, nq, nk, nv) = jnp.split(h @ w_in, splits, axis=-1)
    o_gla = gla_mixer(gq, gk, gv, gg, ga_f, ga_b, gla_w_a2_fwd, gla_b_fwd, gla_w_a2_bwd, gla_b_bwd, gla_out_norm)
    o_gmlp = gmlp_mixer(mu, mv, gmlp_ln_g, gmlp_ln_b, gmlp_w_s, gmlp_b_s)
    o_na = na_mixer(nq, nk, nv, na_rpb)
    g_gla, g_gmlp, g_na = jnp.split(jax.nn.sigmoid(h @ w_gate), N_BRANCHES, axis=-1)
    wb_gla, wb_gmlp, wb_na = jnp.split(w_branch, [GLA_WIDTH, GLA_WIDTH + GMLP_WIDTH], axis=0)
    y = g_gla * (o_gla @ wb_gla) + g_gmlp * (o_gmlp @ wb_gmlp) + g_na * (o_na @ wb_na)
    return y @ w_out


def setup_inputs(seed: int = 0) -> dict:
    key = jax.random.key(seed)
    ks = iter(jax.random.split(key, 32))
    f32 = jnp.float32
    L = DEPTH

    def w(shape, fan_in):
        return jax.random.normal(next(ks), shape, f32) * fan_in ** -0.5

    def gain(shape):
        return 1.0 + 0.05 * jax.random.normal(next(ks), shape, f32)

    def small(shape, s):
        return s * jax.random.normal(next(ks), shape, f32)

    return {
        "x": jax.random.normal(next(ks), (BATCH, SEQ, D_MODEL), f32),
        "ffn1_norm": gain((L, D_MODEL)),
        "ffn1_w_in": w((L, D_MODEL, 2 * D_FF), D_MODEL),
        "ffn1_w_out": w((L, D_FF, D_MODEL), D_FF),
        "mix_norm": gain((L, D_MODEL)),
        "w_in": w((L, D_MODEL, IN_WIDTH), D_MODEL),
        "w_gate": w((L, D_MODEL, N_BRANCHES * D_MODEL), D_MODEL),
        "gla_w_a2_fwd": w((L, GLA_RANK, GLA_K_WIDTH), GLA_RANK),
        "gla_b_fwd": small((L, GLA_K_WIDTH), 0.1),
        "gla_w_a2_bwd": w((L, GLA_RANK, GLA_K_WIDTH), GLA_RANK),
        "gla_b_bwd": small((L, GLA_K_WIDTH), 0.1),
        "gla_out_norm": gain((L, GLA_DV)),
        "gmlp_ln_g": gain((L, GMLP_WIDTH)),
        "gmlp_ln_b": small((L, GMLP_WIDTH), 0.02),
        "gmlp_w_s": w((L, GMLP_GROUPS, GMLP_CHUNK, GMLP_CHUNK), GMLP_CHUNK),
        "gmlp_b_s": gain((L, GMLP_GROUPS, GMLP_CHUNK)),
        "na_rpb": small((L, NA_HEADS, 2 * NA_WIN_ROWS - 1, 2 * NA_WIN_COLS - 1), 0.1),
        "w_branch": w((L, MIX_WIDTH, D_MODEL), MIX_WIDTH),
        "w_out": w((L, D_MODEL, D_MODEL), D_MODEL),
        "ffn2_norm": gain((L, D_MODEL)),
        "ffn2_w_in": w((L, D_MODEL, 2 * D_FF), D_MODEL),
        "ffn2_w_out": w((L, D_FF, D_MODEL), D_FF),
        "final_norm": gain((D_MODEL,)),
    }


def reference(x, ffn1_norm, ffn1_w_in, ffn1_w_out, mix_norm, w_in, w_gate, gla_w_a2_fwd, gla_b_fwd,
              gla_w_a2_bwd, gla_b_bwd, gla_out_norm, gmlp_ln_g, gmlp_ln_b, gmlp_w_s, gmlp_b_s, na_rpb,
              w_branch, w_out, ffn2_norm, ffn2_w_in, ffn2_w_out, final_norm):
    for l in range(DEPTH):
        x = x + 0.5 * swiglu_ffn(rms_norm(x, ffn1_norm[l]), ffn1_w_in[l], ffn1_w_out[l])
        x = x + hybrid_mixer(rms_norm(x, mix_norm[l]), w_in[l], w_gate[l],
                             gla_w_a2_fwd[l], gla_b_fwd[l], gla_w_a2_bwd[l], gla_b_bwd[l], gla_out_norm[l],
                             gmlp_ln_g[l], gmlp_ln_b[l], gmlp_w_s[l], gmlp_b_s[l], na_rpb[l],
                             w_branch[l], w_out[l])
        x = x + 0.5 * swiglu_ffn(rms_norm(x, ffn2_norm[l]), ffn2_w_in[l], ffn2_w_out[l])
    return rms_norm(x, final_norm)
```

```python
import functools
import math

import numpy as np
import jax
import jax.numpy as jnp
from jax import lax
from jax.experimental import pallas as pl
from jax.experimental.pallas import tpu as pltpu

F32 = jnp.float32
BF16 = jnp.bfloat16
EPS = 1e-6

GRID_W = 64
GLA_HEADS, GLA_DK, GLA_DV, GLA_RANK, GLA_TAU, GLA_CHUNK = 4, 192, 384, 16, 16.0, 64
GLA_DKP = 256
GMLP_GROUPS, GMLP_GROUP_DIM, GMLP_CHUNK = 8, 128, 128
GMLP_WIDTH = GMLP_GROUPS * GMLP_GROUP_DIM
NA_HEADS, NA_HEAD_DIM, NA_WIN_ROWS, NA_WIN_COLS = 12, 128, 8, 16
NA_WIDTH = NA_HEADS * NA_HEAD_DIM
GLA_K_WIDTH = GLA_HEADS * GLA_DK
GLA_WIDTH = GLA_HEADS * GLA_DV
NA_QROWS = 4
NA_KROWS = NA_QROWS + NA_WIN_ROWS

OFF_GQ, OFF_GK, OFF_MU, OFF_GV, OFF_GG = 0, 1024, 2048, 3072, 4608
OFF_NQ, OFF_NK, OFF_NV, OFF_A, OFF_MV = 6144, 7680, 9216, 10752, 11264
PROJ_WIDTH = 12288
LANE = 128

VMEM_LIMIT = 56 * 1024 * 1024


def _round_up(x, m):
    return (x + m - 1) // m * m


def _tile(dim, pref):
    if dim <= pref:
        return dim
    t = pref - pref % LANE
    while dim % t:
        t -= LANE
    return t


def _params(*sem):
    return pltpu.CompilerParams(dimension_semantics=sem, vmem_limit_bytes=VMEM_LIMIT)


def _rmsnorm_body(x_ref, g_ref, o_ref):
    x = x_ref[...]
    ms = jnp.mean(x * x, axis=-1, keepdims=True)
    o_ref[...] = (x * lax.rsqrt(ms + EPS) * g_ref[...]).astype(o_ref.dtype)


def _rmsnorm(x, g, out_dtype):
    m, d = x.shape
    bm = _tile(m, 128)
    return pl.pallas_call(
        _rmsnorm_body,
        grid=(m // bm,),
        in_specs=[pl.BlockSpec((bm, d), lambda i: (i, 0)),
                  pl.BlockSpec((1, d), lambda i: (0, 0))],
        out_specs=pl.BlockSpec((bm, d), lambda i: (i, 0)),
        out_shape=jax.ShapeDtypeStruct((m, d), out_dtype),
        compiler_params=_params("arbitrary"),
        name="rmsnorm",
    )(x, g.reshape(1, d))


def _linear_body(x_ref, w_ref, o_ref, *, act):
    acc = jnp.dot(x_ref[...], w_ref[...], preferred_element_type=F32)
    if act == "sigmoid":
        acc = jax.nn.sigmoid(acc)
    o_ref[...] = acc.astype(o_ref.dtype)


def _linear(x, w, act, name):
    m, k = x.shape
    n = w.shape[1]
    bm, bn = _tile(m, 1024), _tile(n, 1024)
    return pl.pallas_call(
        functools.partial(_linear_body, act=act),
        grid=(m // bm, n // bn),
        in_specs=[pl.BlockSpec((bm, k), lambda i, j: (i, 0)),
                  pl.BlockSpec((k, bn), lambda i, j: (0, j))],
        out_specs=pl.BlockSpec((bm, bn), lambda i, j: (i, j)),
        out_shape=jax.ShapeDtypeStruct((m, n), BF16),
        compiler_params=_params("arbitrary", "arbitrary"),
        name=name,
    )(x, w)


def _ffn_in_body(h_ref, wa_ref, wb_ref, o_ref):
    h = h_ref[...]
    a = jnp.dot(h, wa_ref[...], preferred_element_type=F32)
    b = jnp.dot(h, wb_ref[...], preferred_element_type=F32)
    o_ref[...] = (a * jax.nn.sigmoid(a) * b).astype(o_ref.dtype)


def _ffn_in(h, w):
    m, k = h.shape
    f = w.shape[1] // 2
    bm, bn = _tile(m, 1024), _tile(f, 512)
    nb = f // bn
    return pl.pallas_call(
        _ffn_in_body,
        grid=(m // bm, nb),
        in_specs=[pl.BlockSpec((bm, k), lambda i, j: (i, 0)),
                  pl.BlockSpec((k, bn), lambda i, j: (0, j)),
                  pl.BlockSpec((k, bn), lambda i, j: (0, j + nb))],
        out_specs=pl.BlockSpec((bm, bn), lambda i, j: (i, j)),
        out_shape=jax.ShapeDtypeStruct((m, f), BF16),
        compiler_params=_params("arbitrary", "arbitrary"),
        name="ffn_in",
    )(h, w, w)


def _residual_body(a_ref, w_ref, x_ref, o_ref, *, scale):
    part = scale * jnp.dot(a_ref[...], w_ref[...], preferred_element_type=F32)

    @pl.when(pl.program_id(2) == 0)
    def _():
        o_ref[...] = x_ref[...] + part

    @pl.when(pl.program_id(2) > 0)
    def _():
        o_ref[...] += part


def _residual_linear(a, w, x, scale, bn_pref, tk_pref, name):
    m, k = a.shape
    n = w.shape[1]
    bm, bn, tk = _tile(m, 1024), _tile(n, bn_pref), _tile(k, tk_pref)
    return pl.pallas_call(
        functools.partial(_residual_body, scale=scale),
        grid=(m // bm, n // bn, k // tk),
        in_specs=[pl.BlockSpec((bm, tk), lambda i, j, kk: (i, kk)),
                  pl.BlockSpec((tk, bn), lambda i, j, kk: (kk, j)),
                  pl.BlockSpec((bm, bn), lambda i, j, kk: (i, j))],
        out_specs=pl.BlockSpec((bm, bn), lambda i, j, kk: (i, j)),
        out_shape=jax.ShapeDtypeStruct((m, n), F32),
        compiler_params=_params("arbitrary", "arbitrary", "arbitrary"),
        name=name,
    )(a, w, x)


def _merge_body(og_ref, om_ref, on_ref, wg_ref, wm_ref, wn_ref, g1_ref, g2_ref, g3_ref, y_ref):
    y = g1_ref[...].astype(F32) * jnp.dot(og_ref[...], wg_ref[...], preferred_element_type=F32)
    y += g2_ref[...].astype(F32) * jnp.dot(om_ref[...], wm_ref[...], preferred_element_type=F32)
    y += g3_ref[...].astype(F32) * jnp.dot(on_ref[...], wn_ref[...], preferred_element_type=F32)
    y_ref[...] = y.astype(y_ref.dtype)


def _merge(o_gla, o_gmlp, o_na, wb_gla, wb_gmlp, wb_na, gates):
    m = o_gla.shape[0]
    d = wb_gla.shape[1]
    bm, bn = _tile(m, 1024), _tile(d, 512)
    nb = d // bn
    row = lambda i, j: (i, 0)
    col = lambda i, j: (0, j)
    return pl.pallas_call(
        _merge_body,
        grid=(m // bm, nb),
        in_specs=[pl.BlockSpec((bm, o_gla.shape[1]), row),
                  pl.BlockSpec((bm, o_gmlp.shape[1]), row),
                  pl.BlockSpec((bm, o_na.shape[1]), row),
                  pl.BlockSpec((wb_gla.shape[0], bn), col),
                  pl.BlockSpec((wb_gmlp.shape[0], bn), col),
                  pl.BlockSpec((wb_na.shape[0], bn), col),
                  pl.BlockSpec((bm, bn), lambda i, j: (i, j)),
                  pl.BlockSpec((bm, bn), lambda i, j: (i, j + nb)),
                  pl.BlockSpec((bm, bn), lambda i, j: (i, j + 2 * nb))],
        out_specs=pl.BlockSpec((bm, bn), lambda i, j: (i, j)),
        out_shape=jax.ShapeDtypeStruct((m, d), BF16),
        compiler_params=_params("arbitrary", "arbitrary"),
        name="branch_merge",
    )(o_gla, o_gmlp, o_na, wb_gla, wb_gmlp, wb_na, gates, gates, gates)


GLA_ROWS = 256


def _log_sigmoid(x):
    return jnp.minimum(x, 0.0) - jnp.log1p(jnp.exp(-jnp.abs(x)))


def _gla_body(q_ref, k_ref, v_ref, g_ref, a_ref, w2f_ref, w2b_ref, bf_ref, bb_ref, gn_ref, o_ref,
              cf_s, cb_s, acc_s, sf_s, sb_s):
    seq = q_ref.shape[0]
    c = GLA_CHUNK
    n_chunks = seq // c
    rb = min(GLA_ROWS, seq)

    r = lax.broadcasted_iota(jnp.int32, (rb, rb), 0)
    cc = lax.broadcasted_iota(jnp.int32, (rb, rb), 1)
    same = lax.shift_right_logical(r, 6) == lax.shift_right_logical(cc, 6)
    tri_f = jnp.where(same & (cc <= r), 1.0, 0.0).astype(F32)
    tri_b = jnp.where(same & (cc >= r), 1.0, 0.0).astype(F32)

    def decay_step(i, carry):
        rows = pl.ds(pl.multiple_of(i * rb, rb), rb)
        a = a_ref[rows, :]
        la_f = _log_sigmoid(jnp.dot(a, w2f_ref[0], preferred_element_type=F32) + bf_ref[0]) * (1.0 / GLA_TAU)
        la_b = _log_sigmoid(jnp.dot(a, w2b_ref[0], preferred_element_type=F32) + bb_ref[0]) * (1.0 / GLA_TAU)
        cf_s[rows, :] = jnp.dot(tri_f, la_f, preferred_element_type=F32, precision=lax.Precision.HIGHEST)
        cb_s[rows, :] = jnp.dot(tri_b, la_b, preferred_element_type=F32, precision=lax.Precision.HIGHEST)
        acc_s[rows, :] = jnp.zeros((rb, GLA_DV), F32)
        return carry

    lax.fori_loop(0, seq // rb, decay_step, 0)
    sf_s[...] = jnp.zeros(sf_s.shape, F32)
    sb_s[...] = jnp.zeros(sb_s.shape, F32)

    ii = lax.broadcasted_iota(jnp.int32, (c, c), 0)
    jj = lax.broadcasted_iota(jnp.int32, (c, c), 1)
    mask_f = jj <= ii
    mask_b = jj > ii
    q_scale = GLA_DK ** -0.5

    def chunk(idx, cum_s, state_s, mask, mid, last):
        rows = pl.ds(pl.multiple_of(idx * c, c), c)
        cum = cum_s[rows, :]
        b_mid = cum[mid:mid + 1, :]
        b_last = cum[last:last + 1, :]
        q = q_ref[rows, :].astype(F32) * q_scale
        k = k_ref[rows, :].astype(F32)
        v = v_ref[rows, :]
        q_intra = (q * jnp.exp(cum - b_mid)).astype(BF16)
        k_intra = (k * jnp.exp(b_mid - cum)).astype(BF16)
        scores = lax.dot_general(q_intra, k_intra, (((1,), (1,)), ((), ())), preferred_element_type=F32)
        scores = jnp.where(mask, scores, 0.0).astype(BF16)
        o = jnp.dot(scores, v, preferred_element_type=F32)
        state = state_s[...]
        q_inter = (q * jnp.exp(cum)).astype(BF16)
        o += lax.dot_general(q_inter, state.astype(BF16), (((1,), (1,)), ((), ())), preferred_element_type=F32)
        k_state = (k * jnp.exp(b_last - cum)).astype(BF16)
        kv = lax.dot_general(v, k_state, (((0,), (0,)), ((), ())), preferred_element_type=F32)
        state_s[...] = state * jnp.exp(b_last) + kv
        acc_s[rows, :] += o

    def scan_step(n, carry):
        chunk(n, cf_s, sf_s, mask_f, c // 2 - 1, c - 1)
        chunk(n_chunks - 1 - n, cb_s, sb_s, mask_b, c // 2, 0)
        return carry

    lax.fori_loop(0, n_chunks, scan_step, 0)

    def out_step(i, carry):
        rows = pl.ds(pl.multiple_of(i * rb, rb), rb)
        o = acc_s[rows, :]
        ms = jnp.mean(o * o, axis=-1, keepdims=True)
        g = g_ref[rows, :].astype(F32)
        o_ref[rows, :] = (o * lax.rsqrt(ms + EPS) * gn_ref[...] * (g * jax.nn.sigmoid(g))).astype(o_ref.dtype)
        return carry

    lax.fori_loop(0, seq // rb, out_step, 0)


def _gla(proj, w2f, w2b, bias_f, bias_b, out_norm, batch, seq):
    m = proj.shape[0]
    kb, vb = GLA_DKP, GLA_DV
    hd = GLA_HEADS
    return pl.pallas_call(
        _gla_body,
        grid=(batch, hd),
        in_specs=[pl.BlockSpec((seq, kb), lambda b, h: (b, OFF_GQ // kb + h)),
                  pl.BlockSpec((seq, kb), lambda b, h: (b, OFF_GK // kb + h)),
                  pl.BlockSpec((seq, vb), lambda b, h: (b, OFF_GV // vb + h)),
                  pl.BlockSpec((seq, vb), lambda b, h: (b, OFF_GG // vb + h)),
                  pl.BlockSpec((seq, LANE), lambda b, h: (b, OFF_A // LANE)),
                  pl.BlockSpec((1, LANE, kb), lambda b, h: (h, 0, 0)),
                  pl.BlockSpec((1, LANE, kb), lambda b, h: (h, 0, 0)),
                  pl.BlockSpec((1, 1, kb), lambda b, h: (h, 0, 0)),
                  pl.BlockSpec((1, 1, kb), lambda b, h: (h, 0, 0)),
                  pl.BlockSpec((1, vb), lambda b, h: (0, 0))],
        out_specs=pl.BlockSpec((seq, vb), lambda b, h: (b, h)),
        out_shape=jax.ShapeDtypeStruct((m, GLA_WIDTH), BF16),
        scratch_shapes=[pltpu.VMEM((seq, kb), F32), pltpu.VMEM((seq, kb), F32),
                        pltpu.VMEM((seq, vb), F32),
                        pltpu.VMEM((vb, kb), F32), pltpu.VMEM((vb, kb), F32)],
        compiler_params=_params("arbitrary", "arbitrary"),
        name="gla_mixer",
    )(proj, proj, proj, proj, proj, w2f, w2b, bias_f, bias_b, out_norm.reshape(1, vb))


GMLP_ROWS = 512


def _gelu_tanh(x):
    return 0.5 * x * (1.0 + jnp.tanh(math.sqrt(2.0 / math.pi) * (x + 0.044715 * (x * x * x))))


def _gmlp_body(u_ref, v_ref, lg_ref, lb_ref, ws_ref, bs_ref, o_ref):
    t = GMLP_CHUNK
    for ci in range(u_ref.shape[0] // t):
        rows = slice(ci * t, (ci + 1) * t)
        vf = _gelu_tanh(v_ref[rows, :].astype(F32))
        mu = jnp.mean(vf, axis=-1, keepdims=True)
        var = jnp.mean(jnp.square(vf - mu), axis=-1, keepdims=True)
        vn = ((vf - mu) * lax.rsqrt(var + EPS) * lg_ref[...] + lb_ref[...]).astype(BF16)
        for g in range(GMLP_GROUPS):
            cols = slice(g * GMLP_GROUP_DIM, (g + 1) * GMLP_GROUP_DIM)
            s = jnp.dot(ws_ref[g], vn[:, cols], preferred_element_type=F32) + bs_ref[:, cols]
            u = _gelu_tanh(u_ref[rows, cols].astype(F32))
            o_ref[rows, cols] = (u * s).astype(o_ref.dtype)


def _gmlp(proj, ln_g, ln_b, w_s, bias):
    m = proj.shape[0]
    w = GMLP_WIDTH
    bm = _tile(m, GMLP_ROWS)
    return pl.pallas_call(
        _gmlp_body,
        grid=(m // bm,),
        in_specs=[pl.BlockSpec((bm, w), lambda i: (i, OFF_MU // w)),
                  pl.BlockSpec((bm, w), lambda i: (i, OFF_MV // w)),
                  pl.BlockSpec((1, w), lambda i: (0, 0)),
                  pl.BlockSpec((1, w), lambda i: (0, 0)),
                  pl.BlockSpec((GMLP_GROUPS, GMLP_CHUNK, GMLP_CHUNK), lambda i: (0, 0, 0)),
                  pl.BlockSpec((GMLP_CHUNK, w), lambda i: (0, 0))],
        out_specs=pl.BlockSpec((bm, w), lambda i: (i, 0)),
        out_shape=jax.ShapeDtypeStruct((m, w), BF16),
        compiler_params=_params("arbitrary"),
        name="gmlp_mixer",
    )(proj, proj, ln_g.reshape(1, w), ln_b.reshape(1, w), w_s, bias)


def _na_body(q_ref, k_ref, v_ref, bias_ref, o_ref, *, rows):
    qb = pl.program_id(2)
    start = jnp.clip(qb * NA_QROWS - NA_WIN_ROWS // 2, 0, rows - NA_KROWS) * GRID_W
    keys = pl.ds(pl.multiple_of(start, GRID_W), NA_KROWS * GRID_W)
    s = lax.dot_general(q_ref[...], k_ref[keys, :], (((1,), (1,)), ((), ())), preferred_element_type=F32)
    s = s * (NA_HEAD_DIM ** -0.5) + bias_ref[0, 0]
    p = jnp.exp(s - jnp.max(s, axis=-1, keepdims=True))
    l = jnp.sum(p, axis=-1, keepdims=True)
    o = jnp.dot(p.astype(BF16), v_ref[keys, :], preferred_element_type=F32)
    o_ref[...] = (o / l).astype(o_ref.dtype)


def _na_bias_table(rpb, rows):
    nqb = rows // NA_QROWS
    tables = []
    for qb in (0, 1, nqb - 1):
        start = int(np.clip(qb * NA_QROWS - NA_WIN_ROWS // 2, 0, rows - NA_KROWS))
        qr = qb * NA_QROWS + np.arange(NA_QROWS)[:, None, None, None]
        qc = np.arange(GRID_W)[None, :, None, None]
        kr = start + np.arange(NA_KROWS)[None, None, :, None]
        kc = np.arange(GRID_W)[None, None, None, :]
        r0 = np.clip(qr - NA_WIN_ROWS // 2, 0, rows - NA_WIN_ROWS)
        c0 = np.clip(qc - NA_WIN_COLS // 2, 0, GRID_W - NA_WIN_COLS)
        valid = (kr >= r0) & (kr < r0 + NA_WIN_ROWS) & (kc >= c0) & (kc < c0 + NA_WIN_COLS)
        dy = np.clip(kr - qr + NA_WIN_ROWS - 1, 0, 2 * NA_WIN_ROWS - 2)
        dx = np.clip(kc - qc + NA_WIN_COLS - 1, 0, 2 * NA_WIN_COLS - 2)
        shape = (NA_QROWS * GRID_W, NA_KROWS * GRID_W)
        dy = np.broadcast_to(dy, valid.shape).reshape(shape)
        dx = np.broadcast_to(dx, valid.shape).reshape(shape)
        bias = rpb[:, dy, dx]
        tables.append(jnp.where(valid.reshape(shape)[None], bias, -jnp.inf))
    return jnp.stack(tables, axis=1).astype(F32)


def _na(proj, bias_table, batch, seq):
    m = proj.shape[0]
    rows = seq // GRID_W
    nqb = rows // NA_QROWS
    qn = NA_QROWS * GRID_W
    kn = NA_KROWS * GRID_W
    d = NA_HEAD_DIM

    def bias_map(b, h, qb):
        return (h, jnp.minimum(qb, 1) + qb // (nqb - 1), 0, 0)

    return pl.pallas_call(
        functools.partial(_na_body, rows=rows),
        grid=(batch, NA_HEADS, nqb),
        in_specs=[pl.BlockSpec((qn, d), lambda b, h, qb: (b * nqb + qb, OFF_NQ // d + h)),
                  pl.BlockSpec((seq, d), lambda b, h, qb: (b, OFF_NK // d + h)),
                  pl.BlockSpec((seq, d), lambda b, h, qb: (b, OFF_NV // d + h)),
                  pl.BlockSpec((1, 1, qn, kn), bias_map)],
        out_specs=pl.BlockSpec((qn, d), lambda b, h, qb: (b * nqb + qb, h)),
        out_shape=jax.ShapeDtypeStruct((m, NA_WIDTH), BF16),
        compiler_params=_params("arbitrary", "arbitrary", "arbitrary"),
        name="na_mixer",
    )(proj, proj, proj, bias_table)


def _pack_w_in(w):
    d = w.shape[0]
    sizes = (GLA_K_WIDTH, GLA_K_WIDTH, GLA_WIDTH, GLA_WIDTH, GLA_RANK, GLA_RANK,
             GMLP_WIDTH, GMLP_WIDTH, NA_WIDTH, NA_WIDTH, NA_WIDTH)
    splits = np.cumsum(sizes)[:-1].tolist()
    gq, gk, gv, gg, ga_f, ga_b, mu, mv, nq, nk, nv = jnp.split(w, splits, axis=1)

    def pad_heads(x):
        x = x.reshape(d, GLA_HEADS, GLA_DK)
        x = jnp.pad(x, ((0, 0), (0, 0), (0, GLA_DKP - GLA_DK)))
        return x.reshape(d, GLA_HEADS * GLA_DKP)

    a_pad = jnp.zeros((d, OFF_MV - OFF_A - 2 * GLA_RANK), w.dtype)
    packed = jnp.concatenate([pad_heads(gq), pad_heads(gk), mu, gv, gg, nq, nk, nv, ga_f, ga_b, a_pad, mv], axis=1)
    assert packed.shape[1] == PROJ_WIDTH
    return packed.astype(BF16)


def _pack_decay(w2, bias, slot):
    w = w2.reshape(GLA_RANK, GLA_HEADS, GLA_DK).transpose(1, 0, 2)
    w = jnp.pad(w, ((0, 0), (slot * GLA_RANK, LANE - (slot + 1) * GLA_RANK), (0, GLA_DKP - GLA_DK)))
    b = jnp.pad(bias.reshape(GLA_HEADS, 1, GLA_DK), ((0, 0), (0, 0), (0, GLA_DKP - GLA_DK)))
    return w.astype(BF16), b.astype(F32)


def _pack_ffn(w_in, w_out):
    f = w_out.shape[0]
    fp = _round_up(f, 1024)
    a, b = w_in[:, :f], w_in[:, f:]
    pad = ((0, 0), (0, fp - f))
    w_in_p = jnp.concatenate([jnp.pad(a, pad), jnp.pad(b, pad)], axis=1).astype(BF16)
    w_out_p = jnp.pad(w_out, ((0, fp - f), (0, 0))).astype(BF16)
    return w_in_p, w_out_p


def _ffn(x, norm_g, w_in, w_out):
    w_in_p, w_out_p = _pack_ffn(w_in, w_out)
    h = _rmsnorm(x, norm_g, BF16)
    act = _ffn_in(h, w_in_p)
    return _residual_linear(act, w_out_p, x, 0.5, 1024, 2816, "ffn_out")


def _mixer(x, batch, seq, norm_g, w_in, w_gate, w_a2_f, b_f, w_a2_b, b_b, out_norm,
           ln_g, ln_b, w_s, b_s, rpb, w_branch, w_out):
    h = _rmsnorm(x, norm_g, BF16)
    proj = _linear(h, _pack_w_in(w_in), None, "mixer_in_proj")
    gates = _linear(h, w_gate.astype(BF16), "sigmoid", "mixer_gates")

    w2f, bias_f = _pack_decay(w_a2_f, b_f, 0)
    w2b, bias_b = _pack_decay(w_a2_b, b_b, 1)
    o_gla = _gla(proj, w2f, w2b, bias_f, bias_b, out_norm, batch, seq)

    gmlp_bias = jnp.repeat(b_s.T, GMLP_GROUP_DIM, axis=1)
    o_gmlp = _gmlp(proj, ln_g, ln_b, w_s.astype(BF16), gmlp_bias)

    o_na = _na(proj, _na_bias_table(rpb, seq // GRID_W), batch, seq)

    wb = w_branch.astype(BF16)
    y = _merge(o_gla, o_gmlp, o_na, wb[:GLA_WIDTH], wb[GLA_WIDTH:GLA_WIDTH + GMLP_WIDTH],
               wb[GLA_WIDTH + GMLP_WIDTH:], gates)
    return _residual_linear(y, w_out.astype(BF16), x, 1.0, 512, 4096, "mixer_out")


def kernel(x, ffn1_norm, ffn1_w_in, ffn1_w_out, mix_norm, w_in, w_gate, gla_w_a2_fwd, gla_b_fwd, gla_w_a2_bwd, gla_b_bwd, gla_out_norm, gmlp_ln_g, gmlp_ln_b, gmlp_w_s, gmlp_b_s, na_rpb, w_branch, w_out, ffn2_norm, ffn2_w_in, ffn2_w_out, final_norm):
    batch, seq, d = x.shape
    assert seq % (NA_QROWS * GRID_W) == 0 and seq // GRID_W >= NA_KROWS and seq % GMLP_CHUNK == 0
    xs = x.reshape(batch * seq, d)
    for l in range(ffn1_norm.shape[0]):
        xs = _ffn(xs, ffn1_norm[l], ffn1_w_in[l], ffn1_w_out[l])
        xs = _mixer(xs, batch, seq, mix_norm[l], w_in[l], w_gate[l], gla_w_a2_fwd[l], gla_b_fwd[l],
                    gla_w_a2_bwd[l], gla_b_bwd[l], gla_out_norm[l], gmlp_ln_g[l], gmlp_ln_b[l],
                    gmlp_w_s[l], gmlp_b_s[l], na_rpb[l], w_branch[l], w_out[l])
        xs = _ffn(xs, ffn2_norm[l], ffn2_w_in[l], ffn2_w_out[l])
    return _rmsnorm(xs, final_norm, F32).reshape(batch, seq, d)
```

```python
import functools
import math

import numpy as np
import jax
import jax.numpy as jnp
from jax import lax
from jax.experimental import pallas as pl
from jax.experimental.pallas import tpu as pltpu

F32 = jnp.float32
BF16 = jnp.bfloat16
EPS = 1e-6

GRID_W = 64
GLA_HEADS, GLA_DK, GLA_DV, GLA_RANK, GLA_TAU, GLA_CHUNK = 4, 192, 384, 16, 16.0, 64
GLA_DKP = 256
GMLP_GROUPS, GMLP_GROUP_DIM, GMLP_CHUNK = 8, 128, 128
GMLP_WIDTH = GMLP_GROUPS * GMLP_GROUP_DIM
NA_HEADS, NA_HEAD_DIM, NA_WIN_ROWS, NA_WIN_COLS = 12, 128, 8, 16
NA_WIDTH = NA_HEADS * NA_HEAD_DIM
GLA_K_WIDTH = GLA_HEADS * GLA_DK
GLA_WIDTH = GLA_HEADS * GLA_DV
NA_QROWS = 4
NA_KROWS = NA_QROWS + NA_WIN_ROWS

OFF_GQ, OFF_GK, OFF_MU, OFF_GV, OFF_GG = 0, 1024, 2048, 3072, 4608
OFF_NQ, OFF_NK, OFF_NV, OFF_A, OFF_MV = 6144, 7680, 9216, 10752, 11264
PROJ_WIDTH = 12288
LANE = 128

VMEM_LIMIT = 56 * 1024 * 1024


def _round_up(x, m):
    return (x + m - 1) // m * m


def _tile(dim, pref):
    if dim <= pref:
        return dim
    t = pref - pref % LANE
    while dim % t:
        t -= LANE
    return t


def _params(*sem):
    return pltpu.CompilerParams(dimension_semantics=sem, vmem_limit_bytes=VMEM_LIMIT)


def _rmsnorm_body(x_ref, g_ref, o_ref):
    x = x_ref[...]
    ms = jnp.mean(x * x, axis=-1, keepdims=True)
    o_ref[...] = (x * lax.rsqrt(ms + EPS) * g_ref[...]).astype(o_ref.dtype)


def _rmsnorm(x, g, out_dtype):
    m, d = x.shape
    bm = _tile(m, 256)
    return pl.pallas_call(
        _rmsnorm_body,
        grid=(m // bm,),
        in_specs=[pl.BlockSpec((bm, d), lambda i: (i, 0)),
                  pl.BlockSpec((1, d), lambda i: (0, 0))],
        out_specs=pl.BlockSpec((bm, d), lambda i: (i, 0)),
        out_shape=jax.ShapeDtypeStruct((m, d), out_dtype),
        compiler_params=_params("arbitrary"),
        name="rmsnorm",
    )(x, g.reshape(1, d))


def _linear_body(x_ref, w_ref, o_ref, *, act):
    acc = jnp.dot(x_ref[...], w_ref[...], preferred_element_type=F32)
    if act == "sigmoid":
        acc = jax.nn.sigmoid(acc)
    o_ref[...] = acc.astype(o_ref.dtype)


def _linear(x, w, act, name):
    m, k = x.shape
    n = w.shape[1]
    bm, bn = _tile(m, 1024), _tile(n, 1024)
    return pl.pallas_call(
        functools.partial(_linear_body, act=act),
        grid=(m // bm, n // bn),
        in_specs=[pl.BlockSpec((bm, k), lambda i, j: (i, 0)),
                  pl.BlockSpec((k, bn), lambda i, j: (0, j))],
        out_specs=pl.BlockSpec((bm, bn), lambda i, j: (i, j)),
        out_shape=jax.ShapeDtypeStruct((m, n), BF16),
        compiler_params=_params("arbitrary", "arbitrary"),
        name=name,
    )(x, w)


def _ffn_in_body(h_ref, wa_ref, wb_ref, o_ref, *, width):
    h = h_ref[...]
    a = jnp.dot(h, wa_ref[...], preferred_element_type=F32)
    b = jnp.dot(h, wb_ref[...], preferred_element_type=F32)
    act = a * jax.nn.sigmoid(a) * b
    bn = o_ref.shape[1]
    col = pl.program_id(1) * bn + lax.broadcasted_iota(jnp.int32, (1, bn), 1)
    o_ref[...] = jnp.where(col < width, act, 0.0).astype(o_ref.dtype)


def _ffn_in(h, wa, wb, f_pad):
    m, k = h.shape
    bm, bn = _tile(m, 1024), _tile(f_pad, 512)
    return pl.pallas_call(
        functools.partial(_ffn_in_body, width=wa.shape[1]),
        grid=(m // bm, f_pad // bn),
        in_specs=[pl.BlockSpec((bm, k), lambda i, j: (i, 0)),
                  pl.BlockSpec((k, bn), lambda i, j: (0, j)),
                  pl.BlockSpec((k, bn), lambda i, j: (0, j))],
        out_specs=pl.BlockSpec((bm, bn), lambda i, j: (i, j)),
        out_shape=jax.ShapeDtypeStruct((m, f_pad), BF16),
        compiler_params=_params("arbitrary", "arbitrary"),
        name="ffn_in",
    )(h, wa, wb)


def _residual_body(a_ref, w_ref, x_ref, o_ref, *, scale):
    part = scale * jnp.dot(a_ref[...], w_ref[...], preferred_element_type=F32)

    @pl.when(pl.program_id(2) == 0)
    def _():
        o_ref[...] = x_ref[...] + part

    @pl.when(pl.program_id(2) > 0)
    def _():
        o_ref[...] += part


def _residual_linear(a, w, x, scale, bn_pref, tk_pref, name):
    m, k = a.shape
    n = w.shape[1]
    bm, bn, tk = _tile(m, 1024), _tile(n, bn_pref), _tile(k, tk_pref)
    return pl.pallas_call(
        functools.partial(_residual_body, scale=scale),
        grid=(m // bm, n // bn, k // tk),
        in_specs=[pl.BlockSpec((bm, tk), lambda i, j, kk: (i, kk)),
                  pl.BlockSpec((tk, bn), lambda i, j, kk: (kk, j)),
                  pl.BlockSpec((bm, bn), lambda i, j, kk: (i, j))],
        out_specs=pl.BlockSpec((bm, bn), lambda i, j, kk: (i, j)),
        out_shape=jax.ShapeDtypeStruct((m, n), F32),
        compiler_params=_params("arbitrary", "arbitrary", "arbitrary"),
        name=name,
    )(a, w, x)


def _merge_body(og_ref, om_ref, on_ref, wg_ref, wm_ref, wn_ref, g1_ref, g2_ref, g3_ref, y_ref):
    y = g1_ref[...].astype(F32) * jnp.dot(og_ref[...], wg_ref[...], preferred_element_type=F32)
    y += g2_ref[...].astype(F32) * jnp.dot(om_ref[...], wm_ref[...], preferred_element_type=F32)
    y += g3_ref[...].astype(F32) * jnp.dot(on_ref[...], wn_ref[...], preferred_element_type=F32)
    y_ref[...] = y.astype(y_ref.dtype)


def _merge(o_gla, o_gmlp, o_na, wb_gla, wb_gmlp, wb_na, gates):
    m = o_gla.shape[0]
    d = wb_gla.shape[1]
    bm, bn = _tile(m, 1024), _tile(d, 512)
    nb = d // bn
    row = lambda i, j: (i, 0)
    col = lambda i, j: (0, j)
    return pl.pallas_call(
        _merge_body,
        grid=(m // bm, nb),
        in_specs=[pl.BlockSpec((bm, o_gla.shape[1]), row),
                  pl.BlockSpec((bm, o_gmlp.shape[1]), row),
                  pl.BlockSpec((bm, o_na.shape[1]), row),
                  pl.BlockSpec((wb_gla.shape[0], bn), col),
                  pl.BlockSpec((wb_gmlp.shape[0], bn), col),
                  pl.BlockSpec((wb_na.shape[0], bn), col),
                  pl.BlockSpec((bm, bn), lambda i, j: (i, j)),
                  pl.BlockSpec((bm, bn), lambda i, j: (i, j + nb)),
                  pl.BlockSpec((bm, bn), lambda i, j: (i, j + 2 * nb))],
        out_specs=pl.BlockSpec((bm, bn), lambda i, j: (i, j)),
        out_shape=jax.ShapeDtypeStruct((m, d), BF16),
        compiler_params=_params("arbitrary", "arbitrary"),
        name="branch_merge",
    )(o_gla, o_gmlp, o_na, wb_gla, wb_gmlp, wb_na, gates, gates, gates)


GLA_ROWS = 256
GLA_HPS = 2


def _log_sigmoid(x):
    return jnp.minimum(x, 0.0) - jnp.log(1.0 + jnp.exp(-jnp.abs(x)))


def _dot_exact01(tri, x):
    hi = x.astype(BF16)
    rest = x - hi.astype(F32)
    mid = rest.astype(BF16)
    lo = (rest - mid.astype(F32)).astype(BF16)
    return (jnp.dot(tri, hi, preferred_element_type=F32) + jnp.dot(tri, mid, preferred_element_type=F32)
            + jnp.dot(tri, lo, preferred_element_type=F32))


def _gla_body(q_ref, k_ref, v_ref, g_ref, a_ref, w2f_ref, w2b_ref, bf_ref, bb_ref, gn_ref, o_ref,
              cf_s, cb_s, acc_s, sf_s, sb_s):
    seq = q_ref.shape[0]
    c = GLA_CHUNK
    kp, dv = GLA_DKP, GLA_DV
    n_chunks = seq // c
    rb = min(GLA_ROWS, seq)

    r = lax.broadcasted_iota(jnp.int32, (rb, rb), 0)
    cc = lax.broadcasted_iota(jnp.int32, (rb, rb), 1)
    same = lax.shift_right_logical(r, 6) == lax.shift_right_logical(cc, 6)
    tri_f = jnp.where(same & (cc <= r), 1.0, 0.0).astype(BF16)
    tri_b = jnp.where(same & (cc >= r), 1.0, 0.0).astype(BF16)

    def decay_step(i, carry):
        rows = pl.ds(pl.multiple_of(i * rb, rb), rb)
        a = a_ref[rows, :]
        la_f = _log_sigmoid(jnp.dot(a, w2f_ref[0], preferred_element_type=F32) + bf_ref[0]) * (1.0 / GLA_TAU)
        la_b = _log_sigmoid(jnp.dot(a, w2b_ref[0], preferred_element_type=F32) + bb_ref[0]) * (1.0 / GLA_TAU)
        cf_s[rows, :] = _dot_exact01(tri_f, la_f)
        cb_s[rows, :] = _dot_exact01(tri_b, la_b)
        acc_s[rows, :] = jnp.zeros((rb, acc_s.shape[1]), F32)
        return carry

    lax.fori_loop(0, seq // rb, decay_step, 0)
    sf_s[...] = jnp.zeros(sf_s.shape, F32)
    sb_s[...] = jnp.zeros(sb_s.shape, F32)

    ii = lax.broadcasted_iota(jnp.int32, (c, c), 0)
    jj = lax.broadcasted_iota(jnp.int32, (c, c), 1)
    mask_f = jj <= ii
    mask_b = jj > ii
    q_scale = GLA_DK ** -0.5

    def chunk(idx, hh, cum_s, state_s, mask, mid, last):
        rows = pl.ds(pl.multiple_of(idx * c, c), c)
        kcols = slice(hh * kp, (hh + 1) * kp)
        vcols = slice(hh * dv, (hh + 1) * dv)
        cum = cum_s[rows, kcols]
        b_mid = cum[mid:mid + 1, :]
        b_last = cum[last:last + 1, :]
        q = q_ref[rows, kcols].astype(F32) * q_scale
        k = k_ref[rows, kcols].astype(F32)
        v = v_ref[rows, vcols]
        q_intra = (q * jnp.exp(cum - b_mid)).astype(BF16)
        k_intra = (k * jnp.exp(b_mid - cum)).astype(BF16)
        scores = lax.dot_general(q_intra, k_intra, (((1,), (1,)), ((), ())), preferred_element_type=F32)
        scores = jnp.where(mask, scores, 0.0).astype(BF16)
        o = jnp.dot(scores, v, preferred_element_type=F32)
        state = state_s[hh]
        q_inter = (q * jnp.exp(cum)).astype(BF16)
        o += lax.dot_general(q_inter, state.astype(BF16), (((1,), (1,)), ((), ())), preferred_element_type=F32)
        k_state = (k * jnp.exp(b_last - cum)).astype(BF16)
        kv = lax.dot_general(v, k_state, (((0,), (0,)), ((), ())), preferred_element_type=F32)
        state_s[hh] = state * jnp.exp(b_last) + kv
        acc_s[rows, vcols] += o

    def scan_step(n, carry):
        for hh in range(sf_s.shape[0]):
            chunk(n, hh, cf_s, sf_s, mask_f, c // 2 - 1, c - 1)
            chunk(n_chunks - 1 - n, hh, cb_s, sb_s, mask_b, c // 2, 0)
        return carry

    lax.fori_loop(0, n_chunks, scan_step, 0)

    def out_step(i, carry):
        rows = pl.ds(pl.multiple_of(i * rb, rb), rb)
        for hh in range(sf_s.shape[0]):
            vcols = slice(hh * dv, (hh + 1) * dv)
            o = acc_s[rows, vcols]
            ms = jnp.mean(o * o, axis=-1, keepdims=True)
            g = g_ref[rows, vcols].astype(F32)
            o_ref[rows, vcols] = (o * lax.rsqrt(ms + EPS) * gn_ref[...] * (g * jax.nn.sigmoid(g))).astype(o_ref.dtype)
        return carry

    lax.fori_loop(0, seq // rb, out_step, 0)


def _gla(proj, w2f, w2b, bias_f, bias_b, out_norm, batch, seq):
    m = proj.shape[0]
    hps = GLA_HPS
    kb, vb = hps * GLA_DKP, hps * GLA_DV
    assert OFF_GQ % kb == 0 and OFF_GK % kb == 0 and OFF_GV % vb == 0 and OFF_GG % vb == 0
    return pl.pallas_call(
        _gla_body,
        grid=(batch, GLA_HEADS // hps),
        in_specs=[pl.BlockSpec((seq, kb), lambda b, h: (b, OFF_GQ // kb + h)),
                  pl.BlockSpec((seq, kb), lambda b, h: (b, OFF_GK // kb + h)),
                  pl.BlockSpec((seq, vb), lambda b, h: (b, OFF_GV // vb + h)),
                  pl.BlockSpec((seq, vb), lambda b, h: (b, OFF_GG // vb + h)),
                  pl.BlockSpec((seq, LANE), lambda b, h: (b, OFF_A // LANE)),
                  pl.BlockSpec((1, LANE, kb), lambda b, h: (h, 0, 0)),
                  pl.BlockSpec((1, LANE, kb), lambda b, h: (h, 0, 0)),
                  pl.BlockSpec((1, 1, kb), lambda b, h: (h, 0, 0)),
                  pl.BlockSpec((1, 1, kb), lambda b, h: (h, 0, 0)),
                  pl.BlockSpec((1, GLA_DV), lambda b, h: (0, 0))],
        out_specs=pl.BlockSpec((seq, vb), lambda b, h: (b, h)),
        out_shape=jax.ShapeDtypeStruct((m, GLA_WIDTH), BF16),
        scratch_shapes=[pltpu.VMEM((seq, kb), F32), pltpu.VMEM((seq, kb), F32),
                        pltpu.VMEM((seq, vb), F32),
                        pltpu.VMEM((hps, GLA_DV, GLA_DKP), F32), pltpu.VMEM((hps, GLA_DV, GLA_DKP), F32)],
        compiler_params=_params("arbitrary", "arbitrary"),
        name="gla_mixer",
    )(proj, proj, proj, proj, proj, w2f, w2b, bias_f, bias_b, out_norm.reshape(1, GLA_DV))


GMLP_ROWS = 512


def _gelu_tanh(x):
    return 0.5 * x * (1.0 + jnp.tanh(math.sqrt(2.0 / math.pi) * (x + 0.044715 * (x * x * x))))


def _gmlp_body(u_ref, v_ref, lg_ref, lb_ref, ws_ref, bs_ref, o_ref):
    t = GMLP_CHUNK
    for ci in range(u_ref.shape[0] // t):
        rows = slice(ci * t, (ci + 1) * t)
        vf = _gelu_tanh(v_ref[rows, :].astype(F32))
        mu = jnp.mean(vf, axis=-1, keepdims=True)
        var = jnp.mean(jnp.square(vf - mu), axis=-1, keepdims=True)
        vn = ((vf - mu) * lax.rsqrt(var + EPS) * lg_ref[...] + lb_ref[...]).astype(BF16)
        for g in range(GMLP_GROUPS):
            cols = slice(g * GMLP_GROUP_DIM, (g + 1) * GMLP_GROUP_DIM)
            s = jnp.dot(ws_ref[g], vn[:, cols], preferred_element_type=F32) + bs_ref[:, cols]
            u = _gelu_tanh(u_ref[rows, cols].astype(F32))
            o_ref[rows, cols] = (u * s).astype(o_ref.dtype)


def _gmlp(proj, ln_g, ln_b, w_s, bias):
    m = proj.shape[0]
    w = GMLP_WIDTH
    bm = _tile(m, GMLP_ROWS)
    return pl.pallas_call(
        _gmlp_body,
        grid=(m // bm,),
        in_specs=[pl.BlockSpec((bm, w), lambda i: (i, OFF_MU // w)),
                  pl.BlockSpec((bm, w), lambda i: (i, OFF_MV // w)),
                  pl.BlockSpec((1, w), lambda i: (0, 0)),
                  pl.BlockSpec((1, w), lambda i: (0, 0)),
                  pl.BlockSpec((GMLP_GROUPS, GMLP_CHUNK, GMLP_CHUNK), lambda i: (0, 0, 0)),
                  pl.BlockSpec((GMLP_CHUNK, w), lambda i: (0, 0))],
        out_specs=pl.BlockSpec((bm, w), lambda i: (i, 0)),
        out_shape=jax.ShapeDtypeStruct((m, w), BF16),
        compiler_params=_params("arbitrary"),
        name="gmlp_mixer",
    )(proj, proj, ln_g.reshape(1, w), ln_b.reshape(1, w), w_s, bias)


def _na_key_start(qb, rows):
    return int(np.clip(qb * NA_QROWS - NA_WIN_ROWS // 2, 0, rows - NA_KROWS))


def _na_table_index(qb, nqb):
    return 0 if qb == 0 else (2 if qb == nqb - 1 else 1)


def _na_body(q_ref, k_ref, v_ref, bias_ref, o_ref, *, rows):
    nqb = rows // NA_QROWS
    qn = NA_QROWS * GRID_W
    for qb in range(nqb):
        start = _na_key_start(qb, rows) * GRID_W
        keys = slice(start, start + NA_KROWS * GRID_W)
        qrows = slice(qb * qn, (qb + 1) * qn)
        s = lax.dot_general(q_ref[qrows, :], k_ref[keys, :], (((1,), (1,)), ((), ())),
                            preferred_element_type=F32)
        s = s * (NA_HEAD_DIM ** -0.5) + bias_ref[0, _na_table_index(qb, nqb)]
        p = jnp.exp(s - jnp.max(s, axis=-1, keepdims=True))
        l = jnp.sum(p, axis=-1, keepdims=True)
        o = jnp.dot(p.astype(BF16), v_ref[keys, :], preferred_element_type=F32)
        o_ref[qrows, :] = (o / l).astype(o_ref.dtype)


def _na_toeplitz(rpb):
    h, ny, nx = rpb.shape
    w = GRID_W
    left = w - NA_WIN_COLS
    v = jnp.pad(rpb, ((0, 0), (0, 0), (left, 2 * w - left - nx)))
    flat = jnp.broadcast_to(v[:, :, None, :], (h, ny, w, 2 * w)).reshape(h, ny, 2 * w * w)
    skew = flat[:, :, : w * (2 * w - 1)].reshape(h, ny, w, 2 * w - 1)
    return skew[..., w - 1:]


def _na_bias_table(rpb, rows):
    nqb = rows // NA_QROWS
    ny = 2 * NA_WIN_ROWS - 1
    toe = jnp.pad(_na_toeplitz(rpb), ((0, 0), (NA_KROWS, NA_KROWS), (0, 0), (0, 0)))
    qc = np.arange(GRID_W)[:, None]
    kc = np.arange(GRID_W)[None, :]
    c0 = np.clip(qc - NA_WIN_COLS // 2, 0, GRID_W - NA_WIN_COLS)
    col_ok = (kc >= c0) & (kc < c0 + NA_WIN_COLS)
    tables = []
    for qb in (0, 1, nqb - 1):
        start = _na_key_start(qb, rows)
        blocks, row_ok = [], []
        for a in range(NA_QROWS):
            qr = qb * NA_QROWS + a
            r0 = int(np.clip(qr - NA_WIN_ROWS // 2, 0, rows - NA_WIN_ROWS))
            kr = start + np.arange(NA_KROWS)
            row_ok.append((kr >= r0) & (kr < r0 + NA_WIN_ROWS))
            dy0 = start - qr + NA_WIN_ROWS - 1
            assert -NA_KROWS <= dy0 <= ny
            blocks.append(lax.slice_in_dim(toe, NA_KROWS + dy0, 2 * NA_KROWS + dy0, axis=1))
        blk = jnp.stack(blocks, axis=1)
        blk = blk.transpose(0, 1, 3, 2, 4)
        ok = np.stack(row_ok)[:, None, :, None] & col_ok[None, :, None, :]
        tab = jnp.where(ok[None], blk, -jnp.inf)
        tables.append(tab.reshape(rpb.shape[0], NA_QROWS * GRID_W, NA_KROWS * GRID_W))
    return jnp.stack(tables, axis=1).astype(F32)


def _na(proj, bias_table, batch, seq):
    m = proj.shape[0]
    rows = seq // GRID_W
    qn = NA_QROWS * GRID_W
    kn = NA_KROWS * GRID_W
    d = NA_HEAD_DIM
    return pl.pallas_call(
        functools.partial(_na_body, rows=rows),
        grid=(NA_HEADS, batch),
        in_specs=[pl.BlockSpec((seq, d), lambda h, b: (b, OFF_NQ // d + h)),
                  pl.BlockSpec((seq, d), lambda h, b: (b, OFF_NK // d + h)),
                  pl.BlockSpec((seq, d), lambda h, b: (b, OFF_NV // d + h)),
                  pl.BlockSpec((1, 3, qn, kn), lambda h, b: (h, 0, 0, 0))],
        out_specs=pl.BlockSpec((seq, d), lambda h, b: (b, h)),
        out_shape=jax.ShapeDtypeStruct((m, NA_WIDTH), BF16),
        compiler_params=_params("arbitrary", "arbitrary"),
        name="na_mixer",
    )(proj, proj, proj, bias_table)


def _pack_w_in(w):
    d = w.shape[0]
    w = w.astype(BF16)
    sizes = (GLA_K_WIDTH, GLA_K_WIDTH, GLA_WIDTH, GLA_WIDTH, GLA_RANK, GLA_RANK,
             GMLP_WIDTH, GMLP_WIDTH, NA_WIDTH, NA_WIDTH, NA_WIDTH)
    splits = np.cumsum(sizes)[:-1].tolist()
    gq, gk, gv, gg, ga_f, ga_b, mu, mv, nq, nk, nv = jnp.split(w, splits, axis=1)

    def pad_heads(x):
        x = x.reshape(d, GLA_HEADS, GLA_DK)
        x = jnp.pad(x, ((0, 0), (0, 0), (0, GLA_DKP - GLA_DK)))
        return x.reshape(d, GLA_HEADS * GLA_DKP)

    a_pad = jnp.zeros((d, OFF_MV - OFF_A - 2 * GLA_RANK), w.dtype)
    packed = jnp.concatenate([pad_heads(gq), pad_heads(gk), mu, gv, gg, nq, nk, nv, ga_f, ga_b, a_pad, mv], axis=1)
    assert packed.shape[1] == PROJ_WIDTH
    return packed


def _pack_decay(w2, bias, slot):
    groups = GLA_HEADS // GLA_HPS
    w = jnp.pad(w2.reshape(GLA_RANK, GLA_HEADS, GLA_DK), ((0, 0), (0, 0), (0, GLA_DKP - GLA_DK)))
    w = w.reshape(GLA_RANK, groups, GLA_HPS * GLA_DKP).transpose(1, 0, 2)
    w = jnp.pad(w, ((0, 0), (slot * GLA_RANK, LANE - (slot + 1) * GLA_RANK), (0, 0)))
    b = jnp.pad(bias.reshape(GLA_HEADS, GLA_DK), ((0, 0), (0, GLA_DKP - GLA_DK)))
    return w.astype(BF16), b.reshape(groups, 1, GLA_HPS * GLA_DKP).astype(F32)


def _pack_ffn(w_in, w_out):
    f = w_out.shape[0]
    fp = _round_up(f, 1024)
    wa, wb = w_in[:, :f].astype(BF16), w_in[:, f:].astype(BF16)
    w_out_p = jnp.pad(w_out.astype(BF16), ((0, fp - f), (0, 0)))
    return wa, wb, w_out_p


def _ffn(x, norm_g, w_in, w_out):
    wa, wb, w_out_p = _pack_ffn(w_in, w_out)
    h = _rmsnorm(x, norm_g, BF16)
    act = _ffn_in(h, wa, wb, w_out_p.shape[0])
    return _residual_linear(act, w_out_p, x, 0.5, 1024, 2816, "ffn_out")


def _mixer(x, batch, seq, norm_g, w_in, w_gate, w_a2_f, b_f, w_a2_b, b_b, out_norm,
           ln_g, ln_b, w_s, b_s, rpb, w_branch, w_out):
    h = _rmsnorm(x, norm_g, BF16)
    proj = _linear(h, _pack_w_in(w_in), None, "mixer_in_proj")
    gates = _linear(h, w_gate.astype(BF16), "sigmoid", "mixer_gates")

    w2f, bias_f = _pack_decay(w_a2_f, b_f, 0)
    w2b, bias_b = _pack_decay(w_a2_b, b_b, 1)
    o_gla = _gla(proj, w2f, w2b, bias_f, bias_b, out_norm, batch, seq)

    gmlp_bias = jnp.repeat(b_s.T, GMLP_GROUP_DIM, axis=1)
    o_gmlp = _gmlp(proj, ln_g, ln_b, w_s.astype(BF16), gmlp_bias)

    o_na = _na(proj, _na_bias_table(rpb, seq // GRID_W), batch, seq)

    wb = w_branch.astype(BF16)
    y = _merge(o_gla, o_gmlp, o_na, wb[:GLA_WIDTH], wb[GLA_WIDTH:GLA_WIDTH + GMLP_WIDTH],
               wb[GLA_WIDTH + GMLP_WIDTH:], gates)
    return _residual_linear(y, w_out.astype(BF16), x, 1.0, 512, 4096, "mixer_out")


def kernel(x, ffn1_norm, ffn1_w_in, ffn1_w_out, mix_norm, w_in, w_gate, gla_w_a2_fwd, gla_b_fwd, gla_w_a2_bwd, gla_b_bwd, gla_out_norm, gmlp_ln_g, gmlp_ln_b, gmlp_w_s, gmlp_b_s, na_rpb, w_branch, w_out, ffn2_norm, ffn2_w_in, ffn2_w_out, final_norm):
    batch, seq, d = x.shape
    assert seq % (NA_QROWS * GRID_W) == 0 and seq // GRID_W >= NA_KROWS and seq % GMLP_CHUNK == 0
    xs = x.reshape(batch * seq, d)
    for l in range(ffn1_norm.shape[0]):
        xs = _ffn(xs, ffn1_norm[l], ffn1_w_in[l], ffn1_w_out[l])
        xs = _mixer(xs, batch, seq, mix_norm[l], w_in[l], w_gate[l], gla_w_a2_fwd[l], gla_b_fwd[l],
                    gla_w_a2_bwd[l], gla_b_bwd[l], gla_out_norm[l], gmlp_ln_g[l], gmlp_ln_b[l],
                    gmlp_w_s[l], gmlp_b_s[l], na_rpb[l], w_branch[l], w_out[l])
        xs = _ffn(xs, ffn2_norm[l], ffn2_w_in[l], ffn2_w_out[l])
    return _rmsnorm(xs, final_norm, F32).reshape(batch, seq, d)
```

```python
import functools
import math

import numpy as np
import jax
import jax.numpy as jnp
from jax import lax
from jax.experimental import pallas as pl
from jax.experimental.pallas import tpu as pltpu

F32 = jnp.float32
BF16 = jnp.bfloat16
EPS = 1e-6

GRID_W = 64
GLA_HEADS, GLA_DK, GLA_DV, GLA_RANK, GLA_TAU, GLA_CHUNK = 4, 192, 384, 16, 16.0, 64
GLA_DKP = 256
GMLP_GROUPS, GMLP_GROUP_DIM, GMLP_CHUNK = 8, 128, 128
GMLP_WIDTH = GMLP_GROUPS * GMLP_GROUP_DIM
NA_HEADS, NA_HEAD_DIM, NA_WIN_ROWS, NA_WIN_COLS = 12, 128, 8, 16
NA_WIDTH = NA_HEADS * NA_HEAD_DIM
GLA_K_WIDTH = GLA_HEADS * GLA_DK
GLA_WIDTH = GLA_HEADS * GLA_DV
NA_QROWS = 4
NA_KROWS = NA_QROWS + NA_WIN_ROWS

OFF_GQ, OFF_GK, OFF_MU, OFF_GV, OFF_GG = 0, 1024, 2048, 3072, 4608
OFF_NQ, OFF_NK, OFF_NV, OFF_A, OFF_MV = 6144, 7680, 9216, 10752, 11264
PROJ_WIDTH = 12288
LANE = 128

VMEM_LIMIT = 56 * 1024 * 1024


def _round_up(x, m):
    return (x + m - 1) // m * m


def _tile(dim, pref):
    if dim <= pref:
        return dim
    t = pref - pref % LANE
    while dim % t:
        t -= LANE
    return t


def _params(*sem):
    return pltpu.CompilerParams(dimension_semantics=sem, vmem_limit_bytes=VMEM_LIMIT)


def _lane_group_sum(x):
    out = x[:, :LANE]
    for c in range(1, x.shape[1] // LANE):
        out = out + x[:, c * LANE:(c + 1) * LANE]
    return out


def _rstd(ss_ref, d):
    return lax.rsqrt(jnp.sum(ss_ref[...], axis=-1, keepdims=True) * (1.0 / d) + EPS)


def _prenorm_body(x_ref, xb_ref, ss_ref):
    x = x_ref[...]
    xb_ref[...] = x.astype(xb_ref.dtype)
    ss_ref[...] = _lane_group_sum(x * x)


def _prenorm(x):
    m, d = x.shape
    bm = _tile(m, 256)
    return pl.pallas_call(
        _prenorm_body,
        grid=(m // bm,),
        in_specs=[pl.BlockSpec((bm, d), lambda i: (i, 0))],
        out_specs=[pl.BlockSpec((bm, d), lambda i: (i, 0)),
                   pl.BlockSpec((bm, LANE), lambda i: (i, 0))],
        out_shape=[jax.ShapeDtypeStruct((m, d), BF16), jax.ShapeDtypeStruct((m, LANE), F32)],
        compiler_params=_params("arbitrary"),
        name="prenorm",
    )(x)


def _rmsnorm_body(x_ref, g_ref, o_ref):
    x = x_ref[...]
    ms = jnp.mean(x * x, axis=-1, keepdims=True)
    o_ref[...] = (x * lax.rsqrt(ms + EPS) * g_ref[...]).astype(o_ref.dtype)


def _rmsnorm(x, g, out_dtype):
    m, d = x.shape
    bm = _tile(m, 256)
    return pl.pallas_call(
        _rmsnorm_body,
        grid=(m // bm,),
        in_specs=[pl.BlockSpec((bm, d), lambda i: (i, 0)),
                  pl.BlockSpec((1, d), lambda i: (0, 0))],
        out_specs=pl.BlockSpec((bm, d), lambda i: (i, 0)),
        out_shape=jax.ShapeDtypeStruct((m, d), out_dtype),
        compiler_params=_params("arbitrary"),
        name="rmsnorm",
    )(x, g.reshape(1, d))


def _normed_linear_body(x_ref, ss_ref, w_ref, o_ref, *, act):
    acc = _rstd(ss_ref, x_ref.shape[1]) * jnp.dot(x_ref[...], w_ref[...], preferred_element_type=F32)
    if act == "sigmoid":
        acc = jax.nn.sigmoid(acc)
    o_ref[...] = acc.astype(o_ref.dtype)


def _normed_linear(xb, ss, w, layer, act, name):
    m, k = xb.shape
    n = w.shape[2]
    bm, bn = _tile(m, 1024), _tile(n, 1024)
    return pl.pallas_call(
        functools.partial(_normed_linear_body, act=act),
        grid=(m // bm, n // bn),
        in_specs=[pl.BlockSpec((bm, k), lambda i, j: (i, 0)),
                  pl.BlockSpec((bm, ss.shape[1]), lambda i, j: (i, 0)),
                  pl.BlockSpec((None, k, bn), lambda i, j: (layer, 0, j))],
        out_specs=pl.BlockSpec((bm, bn), lambda i, j: (i, j)),
        out_shape=jax.ShapeDtypeStruct((m, n), BF16),
        compiler_params=_params("arbitrary", "arbitrary"),
        name=name,
    )(xb, ss, w)


def _ffn_in_body(x_ref, ss_ref, wa_ref, wb0_ref, wb1_ref, o_ref, *, width):
    x = x_ref[...]
    rstd = _rstd(ss_ref, x_ref.shape[1])
    bn = o_ref.shape[1]
    half = bn // 2
    a = rstd * jnp.dot(x, wa_ref[...], preferred_element_type=F32)
    for c, wb_ref in enumerate((wb0_ref, wb1_ref)):
        cols = slice(c * half, (c + 1) * half)
        b = rstd * jnp.dot(x, wb_ref[...], preferred_element_type=F32)
        ac = a[:, cols]
        act = ac * jax.nn.sigmoid(ac) * b
        col = pl.program_id(1) * bn + c * half + lax.broadcasted_iota(jnp.int32, (1, half), 1)
        o_ref[:, cols] = jnp.where(col < width, act, 0.0).astype(o_ref.dtype)


def _ffn_in(xb, ss, w, layer, f_pad):
    m, k = xb.shape
    f = w.shape[2] // 2
    bm, bn = _tile(m, 1024), _tile(f_pad, 512)
    half = bn // 2
    assert f % half == 0
    nb = f // half
    last = 2 * nb - 1
    return pl.pallas_call(
        functools.partial(_ffn_in_body, width=f),
        grid=(m // bm, f_pad // bn),
        in_specs=[pl.BlockSpec((bm, k), lambda i, j: (i, 0)),
                  pl.BlockSpec((bm, ss.shape[1]), lambda i, j: (i, 0)),
                  pl.BlockSpec((None, k, bn), lambda i, j: (layer, 0, j)),
                  pl.BlockSpec((None, k, half), lambda i, j: (layer, 0, jnp.minimum(nb + 2 * j, last))),
                  pl.BlockSpec((None, k, half), lambda i, j: (layer, 0, jnp.minimum(nb + 2 * j + 1, last)))],
        out_specs=pl.BlockSpec((bm, bn), lambda i, j: (i, j)),
        out_shape=jax.ShapeDtypeStruct((m, f_pad), BF16),
        compiler_params=_params("arbitrary", "arbitrary"),
        name="ffn_in",
    )(xb, ss, w, w, w)


def _residual_body(a_ref, w_ref, x_ref, o_ref, *norm_refs, scale):
    part = scale * jnp.dot(a_ref[...], w_ref[...], preferred_element_type=F32)
    nk = pl.num_programs(2)

    @pl.when(pl.program_id(2) == 0)
    def _():
        o_ref[...] = x_ref[...] + part

    @pl.when(pl.program_id(2) > 0)
    def _():
        o_ref[...] += part

    if norm_refs:
        xb_ref, ss_ref = norm_refs

        @pl.when(pl.program_id(2) == nk - 1)
        def _():
            xn = o_ref[...]
            xb_ref[...] = xn.astype(xb_ref.dtype)
            ss_ref[...] = _lane_group_sum(xn * xn)


def _residual_linear(a, w, layer, x, scale, bn_pref, tk_pref, emit_norm, name):
    m, k = a.shape
    n = w.shape[2]
    bm, bn, tk = _tile(m, 1024), _tile(n, bn_pref), _tile(k, tk_pref)
    out_specs = [pl.BlockSpec((bm, bn), lambda i, j, kk: (i, j))]
    out_shape = [jax.ShapeDtypeStruct((m, n), F32)]
    if emit_norm:
        out_specs += [pl.BlockSpec((bm, bn), lambda i, j, kk: (i, j)),
                      pl.BlockSpec((bm, LANE), lambda i, j, kk: (i, j))]
        out_shape += [jax.ShapeDtypeStruct((m, n), BF16), jax.ShapeDtypeStruct((m, LANE * (n // bn)), F32)]
    return pl.pallas_call(
        functools.partial(_residual_body, scale=scale),
        grid=(m // bm, n // bn, k // tk),
        in_specs=[pl.BlockSpec((bm, tk), lambda i, j, kk: (i, kk)),
                  pl.BlockSpec((None, tk, bn), lambda i, j, kk: (layer, kk, j)),
                  pl.BlockSpec((bm, bn), lambda i, j, kk: (i, j))],
        out_specs=out_specs,
        out_shape=out_shape,
        compiler_params=_params("arbitrary", "arbitrary", "arbitrary"),
        name=name,
    )(a, w, x)


MERGE_ROWS = 512


def _merge_body(og_ref, om_ref, on_ref, *refs):
    nblk = (og_ref.shape[1] + om_ref.shape[1] + on_ref.shape[1]) // MERGE_ROWS
    w_refs, (g1_ref, g2_ref, g3_ref, y_ref) = refs[:nblk], refs[nblk:]
    y = None
    first = 0
    for o_ref, g_ref in ((og_ref, g1_ref), (om_ref, g2_ref), (on_ref, g3_ref)):
        cnt = o_ref.shape[1] // MERGE_ROWS
        w = jnp.concatenate([r[...] for r in w_refs[first:first + cnt]], axis=0)
        first += cnt
        term = g_ref[...].astype(F32) * jnp.dot(o_ref[...], w, preferred_element_type=F32)
        y = term if y is None else y + term
    y_ref[...] = y.astype(y_ref.dtype)


def _merge(o_gla, o_gmlp, o_na, w_branch, layer, gates):
    m = o_gla.shape[0]
    kk, d = w_branch.shape[1:]
    bm, bn = _tile(m, 1024), _tile(d, 512)
    nb = d // bn
    row = lambda i, j: (i, 0)
    w_specs = [pl.BlockSpec((None, MERGE_ROWS, bn), functools.partial(lambda i, j, r: (layer, r, j), r=r))
               for r in range(kk // MERGE_ROWS)]
    return pl.pallas_call(
        _merge_body,
        grid=(m // bm, nb),
        in_specs=[pl.BlockSpec((bm, o_gla.shape[1]), row),
                  pl.BlockSpec((bm, o_gmlp.shape[1]), row),
                  pl.BlockSpec((bm, o_na.shape[1]), row)] + w_specs +
                 [pl.BlockSpec((bm, bn), lambda i, j: (i, j)),
                  pl.BlockSpec((bm, bn), lambda i, j: (i, j + nb)),
                  pl.BlockSpec((bm, bn), lambda i, j: (i, j + 2 * nb))],
        out_specs=pl.BlockSpec((bm, bn), lambda i, j: (i, j)),
        out_shape=jax.ShapeDtypeStruct((m, d), BF16),
        compiler_params=_params("arbitrary", "arbitrary"),
        name="branch_merge",
    )(o_gla, o_gmlp, o_na, *([w_branch] * (kk // MERGE_ROWS)), gates, gates, gates)


GLA_ROWS = 256
GLA_HPS = 2


def _log_sigmoid(x):
    return jnp.minimum(x, 0.0) - jnp.log(1.0 + jnp.exp(-jnp.abs(x)))


def _dot_exact01(tri, x):
    hi = x.astype(BF16)
    rest = x - hi.astype(F32)
    mid = rest.astype(BF16)
    lo = (rest - mid.astype(F32)).astype(BF16)
    return (jnp.dot(tri, hi, preferred_element_type=F32) + jnp.dot(tri, mid, preferred_element_type=F32)
            + jnp.dot(tri, lo, preferred_element_type=F32))


def _gla_body(q_ref, k_ref, v_ref, g_ref, a_ref, w2f_ref, w2b_ref, bf_ref, bb_ref, gn_ref, o_ref,
              cf_s, cb_s, acc_s, sf_s, sb_s):
    seq = q_ref.shape[0]
    c = GLA_CHUNK
    kp, dv = GLA_DKP, GLA_DV
    n_chunks = seq // c
    rb = min(GLA_ROWS, seq)

    r = lax.broadcasted_iota(jnp.int32, (rb, rb), 0)
    cc = lax.broadcasted_iota(jnp.int32, (rb, rb), 1)
    same = lax.shift_right_logical(r, 6) == lax.shift_right_logical(cc, 6)
    tri_f = jnp.where(same & (cc <= r), 1.0, 0.0).astype(BF16)
    tri_b = jnp.where(same & (cc >= r), 1.0, 0.0).astype(BF16)

    def decay_step(i, carry):
        rows = pl.ds(pl.multiple_of(i * rb, rb), rb)
        a = a_ref[rows, :]
        la_f = _log_sigmoid(jnp.dot(a, w2f_ref[0], preferred_element_type=F32) + bf_ref[0]) * (1.0 / GLA_TAU)
        la_b = _log_sigmoid(jnp.dot(a, w2b_ref[0], preferred_element_type=F32) + bb_ref[0]) * (1.0 / GLA_TAU)
        cf_s[rows, :] = _dot_exact01(tri_f, la_f)
        cb_s[rows, :] = _dot_exact01(tri_b, la_b)
        acc_s[rows, :] = jnp.zeros((rb, acc_s.shape[1]), F32)
        return carry

    lax.fori_loop(0, seq // rb, decay_step, 0)
    sf_s[...] = jnp.zeros(sf_s.shape, F32)
    sb_s[...] = jnp.zeros(sb_s.shape, F32)

    ii = lax.broadcasted_iota(jnp.int32, (c, c), 0)
    jj = lax.broadcasted_iota(jnp.int32, (c, c), 1)
    mask_f = jj <= ii
    mask_b = jj > ii
    q_scale = GLA_DK ** -0.5

    def chunk(idx, hh, cum_s, state_s, mask, mid, last):
        rows = pl.ds(pl.multiple_of(idx * c, c), c)
        kcols = slice(hh * kp, (hh + 1) * kp)
        vcols = slice(hh * dv, (hh + 1) * dv)
        cum = cum_s[rows, kcols]
        b_mid = cum[mid:mid + 1, :]
        b_last = cum[last:last + 1, :]
        q = q_ref[rows, kcols].astype(F32) * q_scale
        k = k_ref[rows, kcols].astype(F32)
        v = v_ref[rows, vcols]
        q_intra = (q * jnp.exp(cum - b_mid)).astype(BF16)
        k_intra = (k * jnp.exp(b_mid - cum)).astype(BF16)
        scores = lax.dot_general(q_intra, k_intra, (((1,), (1,)), ((), ())), preferred_element_type=F32)
        scores = jnp.where(mask, scores, 0.0).astype(BF16)
        o = jnp.dot(scores, v, preferred_element_type=F32)
        state = state_s[hh]
        q_inter = (q * jnp.exp(cum)).astype(BF16)
        o += lax.dot_general(q_inter, state.astype(BF16), (((1,), (1,)), ((), ())), preferred_element_type=F32)
        k_state = (k * jnp.exp(b_last - cum)).astype(BF16)
        kv = lax.dot_general(v, k_state, (((0,), (0,)), ((), ())), preferred_element_type=F32)
        state_s[hh] = state * jnp.exp(b_last) + kv
        acc_s[rows, vcols] += o

    def scan_step(n, carry):
        for hh in range(sf_s.shape[0]):
            chunk(n, hh, cf_s, sf_s, mask_f, c // 2 - 1, c - 1)
            chunk(n_chunks - 1 - n, hh, cb_s, sb_s, mask_b, c // 2, 0)
        return carry

    lax.fori_loop(0, n_chunks, scan_step, 0)

    def out_step(i, carry):
        rows = pl.ds(pl.multiple_of(i * rb, rb), rb)
        for hh in range(sf_s.shape[0]):
            vcols = slice(hh * dv, (hh + 1) * dv)
            o = acc_s[rows, vcols]
            ms = jnp.mean(o * o, axis=-1, keepdims=True)
            g = g_ref[rows, vcols].astype(F32)
            o_ref[rows, vcols] = (o * lax.rsqrt(ms + EPS) * gn_ref[...] * (g * jax.nn.sigmoid(g))).astype(o_ref.dtype)
        return carry

    lax.fori_loop(0, seq // rb, out_step, 0)


def _gla(proj, w2f, w2b, bias_f, bias_b, out_norm, batch, seq):
    m = proj.shape[0]
    hps = GLA_HPS
    kb, vb = hps * GLA_DKP, hps * GLA_DV
    assert OFF_GQ % kb == 0 and OFF_GK % kb == 0 and OFF_GV % vb == 0 and OFF_GG % vb == 0
    return pl.pallas_call(
        _gla_body,
        grid=(batch, GLA_HEADS // hps),
        in_specs=[pl.BlockSpec((seq, kb), lambda b, h: (b, OFF_GQ // kb + h)),
                  pl.BlockSpec((seq, kb), lambda b, h: (b, OFF_GK // kb + h)),
                  pl.BlockSpec((seq, vb), lambda b, h: (b, OFF_GV // vb + h)),
                  pl.BlockSpec((seq, vb), lambda b, h: (b, OFF_GG // vb + h)),
                  pl.BlockSpec((seq, LANE), lambda b, h: (b, OFF_A // LANE)),
                  pl.BlockSpec((1, LANE, kb), lambda b, h: (h, 0, 0)),
                  pl.BlockSpec((1, LANE, kb), lambda b, h: (h, 0, 0)),
                  pl.BlockSpec((1, 1, kb), lambda b, h: (h, 0, 0)),
                  pl.BlockSpec((1, 1, kb), lambda b, h: (h, 0, 0)),
                  pl.BlockSpec((1, GLA_DV), lambda b, h: (0, 0))],
        out_specs=pl.BlockSpec((seq, vb), lambda b, h: (b, h)),
        out_shape=jax.ShapeDtypeStruct((m, GLA_WIDTH), BF16),
        scratch_shapes=[pltpu.VMEM((seq, kb), F32), pltpu.VMEM((seq, kb), F32),
                        pltpu.VMEM((seq, vb), F32),
                        pltpu.VMEM((hps, GLA_DV, GLA_DKP), F32), pltpu.VMEM((hps, GLA_DV, GLA_DKP), F32)],
        compiler_params=_params("arbitrary", "arbitrary"),
        name="gla_mixer",
    )(proj, proj, proj, proj, proj, w2f, w2b, bias_f, bias_b, out_norm.reshape(1, GLA_DV))


GMLP_ROWS = 512


def _gelu_tanh(x):
    return 0.5 * x * (1.0 + jnp.tanh(math.sqrt(2.0 / math.pi) * (x + 0.044715 * (x * x * x))))


def _gmlp_body(u_ref, v_ref, lg_ref, lb_ref, ws_ref, bs_ref, o_ref):
    t = GMLP_CHUNK
    for ci in range(u_ref.shape[0] // t):
        rows = slice(ci * t, (ci + 1) * t)
        vf = _gelu_tanh(v_ref[rows, :].astype(F32))
        mu = jnp.mean(vf, axis=-1, keepdims=True)
        var = jnp.mean(jnp.square(vf - mu), axis=-1, keepdims=True)
        vn = ((vf - mu) * lax.rsqrt(var + EPS) * lg_ref[...] + lb_ref[...]).astype(BF16)
        for g in range(GMLP_GROUPS):
            cols = slice(g * GMLP_GROUP_DIM, (g + 1) * GMLP_GROUP_DIM)
            s = jnp.dot(ws_ref[g], vn[:, cols], preferred_element_type=F32) + bs_ref[:, cols]
            u = _gelu_tanh(u_ref[rows, cols].astype(F32))
            o_ref[rows, cols] = (u * s).astype(o_ref.dtype)


def _gmlp(proj, ln_g, ln_b, w_s, bias):
    m = proj.shape[0]
    w = GMLP_WIDTH
    bm = _tile(m, GMLP_ROWS)
    return pl.pallas_call(
        _gmlp_body,
        grid=(m // bm,),
        in_specs=[pl.BlockSpec((bm, w), lambda i: (i, OFF_MU // w)),
                  pl.BlockSpec((bm, w), lambda i: (i, OFF_MV // w)),
                  pl.BlockSpec((1, w), lambda i: (0, 0)),
                  pl.BlockSpec((1, w), lambda i: (0, 0)),
                  pl.BlockSpec((GMLP_GROUPS, GMLP_CHUNK, GMLP_CHUNK), lambda i: (0, 0, 0)),
                  pl.BlockSpec((GMLP_CHUNK, w), lambda i: (0, 0))],
        out_specs=pl.BlockSpec((bm, w), lambda i: (i, 0)),
        out_shape=jax.ShapeDtypeStruct((m, w), BF16),
        compiler_params=_params("arbitrary"),
        name="gmlp_mixer",
    )(proj, proj, ln_g.reshape(1, w), ln_b.reshape(1, w), w_s, bias)


def _na_key_start(qb, rows):
    return int(np.clip(qb * NA_QROWS - NA_WIN_ROWS // 2, 0, rows - NA_KROWS))


def _na_table_index(qb, nqb):
    return 0 if qb == 0 else (2 if qb == nqb - 1 else 1)


def _na_body(q_ref, k_ref, v_ref, bias_ref, o_ref, *, rows):
    nqb = rows // NA_QROWS
    qn = NA_QROWS * GRID_W
    for qb in range(nqb):
        start = _na_key_start(qb, rows) * GRID_W
        keys = slice(start, start + NA_KROWS * GRID_W)
        qrows = slice(qb * qn, (qb + 1) * qn)
        s = lax.dot_general(q_ref[qrows, :], k_ref[keys, :], (((1,), (1,)), ((), ())),
                            preferred_element_type=F32)
        s = s * (NA_HEAD_DIM ** -0.5) + bias_ref[0, _na_table_index(qb, nqb)]
        p = jnp.exp(s - jnp.max(s, axis=-1, keepdims=True))
        l = jnp.sum(p, axis=-1, keepdims=True)
        o = jnp.dot(p.astype(BF16), v_ref[keys, :], preferred_element_type=F32)
        o_ref[qrows, :] = (o / l).astype(o_ref.dtype)


def _na_toeplitz(rpb):
    h, ny, nx = rpb.shape
    w = GRID_W
    left = w - NA_WIN_COLS
    v = jnp.pad(rpb, ((0, 0), (0, 0), (left, 2 * w - left - nx)))
    flat = jnp.broadcast_to(v[:, :, None, :], (h, ny, w, 2 * w)).reshape(h, ny, 2 * w * w)
    skew = flat[:, :, : w * (2 * w - 1)].reshape(h, ny, w, 2 * w - 1)
    return skew[..., w - 1:]


def _na_bias_table(rpb, rows):
    nqb = rows // NA_QROWS
    ny = 2 * NA_WIN_ROWS - 1
    toe = jnp.pad(_na_toeplitz(rpb), ((0, 0), (NA_KROWS, NA_KROWS), (0, 0), (0, 0)))
    qc = np.arange(GRID_W)[:, None]
    kc = np.arange(GRID_W)[None, :]
    c0 = np.clip(qc - NA_WIN_COLS // 2, 0, GRID_W - NA_WIN_COLS)
    col_ok = (kc >= c0) & (kc < c0 + NA_WIN_COLS)
    tables = []
    for qb in (0, 1, nqb - 1):
        start = _na_key_start(qb, rows)
        blocks, row_ok = [], []
        for a in range(NA_QROWS):
            qr = qb * NA_QROWS + a
            r0 = int(np.clip(qr - NA_WIN_ROWS // 2, 0, rows - NA_WIN_ROWS))
            kr = start + np.arange(NA_KROWS)
            row_ok.append((kr >= r0) & (kr < r0 + NA_WIN_ROWS))
            dy0 = start - qr + NA_WIN_ROWS - 1
            assert -NA_KROWS <= dy0 <= ny
            blocks.append(lax.slice_in_dim(toe, NA_KROWS + dy0, 2 * NA_KROWS + dy0, axis=1))
        blk = jnp.stack(blocks, axis=1)
        blk = blk.transpose(0, 1, 3, 2, 4)
        ok = np.stack(row_ok)[:, None, :, None] & col_ok[None, :, None, :]
        tab = jnp.where(ok[None], blk, -jnp.inf)
        tables.append(tab.reshape(rpb.shape[0], NA_QROWS * GRID_W, NA_KROWS * GRID_W))
    return jnp.stack(tables, axis=1).astype(F32)


def _na(proj, bias_table, batch, seq):
    m = proj.shape[0]
    rows = seq // GRID_W
    qn = NA_QROWS * GRID_W
    kn = NA_KROWS * GRID_W
    d = NA_HEAD_DIM
    return pl.pallas_call(
        functools.partial(_na_body, rows=rows),
        grid=(NA_HEADS, batch),
        in_specs=[pl.BlockSpec((seq, d), lambda h, b: (b, OFF_NQ // d + h)),
                  pl.BlockSpec((seq, d), lambda h, b: (b, OFF_NK // d + h)),
                  pl.BlockSpec((seq, d), lambda h, b: (b, OFF_NV // d + h)),
                  pl.BlockSpec((1, 3, qn, kn), lambda h, b: (h, 0, 0, 0))],
        out_specs=pl.BlockSpec((seq, d), lambda h, b: (b, h)),
        out_shape=jax.ShapeDtypeStruct((m, NA_WIDTH), BF16),
        compiler_params=_params("arbitrary", "arbitrary"),
        name="na_mixer",
    )(proj, proj, proj, bias_table)


def _pack_moves():
    src_off = np.concatenate([[0], np.cumsum((GLA_K_WIDTH, GLA_K_WIDTH, GLA_WIDTH, GLA_WIDTH, GLA_RANK, GLA_RANK,
                                              GMLP_WIDTH, GMLP_WIDTH, NA_WIDTH, NA_WIDTH, NA_WIDTH))]).tolist()
    s_gq, s_gk, s_gv, s_gg, s_af, s_ab, s_mu, s_mv, s_nq, s_nk, s_nv, _ = src_off
    moves = []
    for h in range(GLA_HEADS):
        moves.append((OFF_GQ + h * GLA_DKP, s_gq + h * GLA_DK, GLA_DK))
        moves.append((OFF_GK + h * GLA_DKP, s_gk + h * GLA_DK, GLA_DK))
    moves += [(OFF_MU, s_mu, GMLP_WIDTH), (OFF_GV, s_gv, GLA_WIDTH), (OFF_GG, s_gg, GLA_WIDTH),
              (OFF_NQ, s_nq, NA_WIDTH), (OFF_NK, s_nk, NA_WIDTH), (OFF_NV, s_nv, NA_WIDTH),
              (OFF_A, s_af, 2 * GLA_RANK), (OFF_MV, s_mv, GMLP_WIDTH)]
    return sorted(moves)


def _pack_in_body(w_ref, g_ref, o_ref):
    g = g_ref[...]
    cursor = 0
    for dst, src, width in _pack_moves():
        if dst > cursor:
            o_ref[:, cursor:dst] = jnp.zeros((o_ref.shape[0], dst - cursor), o_ref.dtype)
        o_ref[:, dst:dst + width] = (w_ref[:, src:src + width] * g).astype(o_ref.dtype)
        cursor = dst + width
    if cursor < o_ref.shape[1]:
        o_ref[:, cursor:] = jnp.zeros((o_ref.shape[0], o_ref.shape[1] - cursor), o_ref.dtype)


def _pack_w_in(w, g):
    depth, d, n = w.shape
    br = _tile(d, 256)
    return pl.pallas_call(
        _pack_in_body,
        grid=(depth, d // br),
        in_specs=[pl.BlockSpec((None, br, n), lambda l, i: (l, i, 0)),
                  pl.BlockSpec((None, br, 1), lambda l, i: (l, i, 0))],
        out_specs=pl.BlockSpec((None, br, PROJ_WIDTH), lambda l, i: (l, i, 0)),
        out_shape=jax.ShapeDtypeStruct((depth, d, PROJ_WIDTH), BF16),
        compiler_params=_params("arbitrary", "arbitrary"),
        name="pack_w_in",
    )(w, g[:, :, None])


def _pack_decay(w2, bias, slot):
    groups = GLA_HEADS // GLA_HPS
    w = jnp.pad(w2.reshape(GLA_RANK, GLA_HEADS, GLA_DK), ((0, 0), (0, 0), (0, GLA_DKP - GLA_DK)))
    w = w.reshape(GLA_RANK, groups, GLA_HPS * GLA_DKP).transpose(1, 0, 2)
    w = jnp.pad(w, ((0, 0), (slot * GLA_RANK, LANE - (slot + 1) * GLA_RANK), (0, 0)))
    b = jnp.pad(bias.reshape(GLA_HEADS, GLA_DK), ((0, 0), (0, GLA_DKP - GLA_DK)))
    return w.astype(BF16), b.reshape(groups, 1, GLA_HPS * GLA_DKP).astype(F32)


def _scaled_bf16(w, g):
    return (w * g[:, :, None]).astype(BF16)


def _pad_rows_bf16(w, rows):
    return jnp.pad(w.astype(BF16), ((0, 0), (0, rows - w.shape[1]), (0, 0)))


def _ffn(x, xb, ss, w_in, w_out_p, layer, emit_norm):
    act = _ffn_in(xb, ss, w_in, layer, w_out_p.shape[1])
    return _residual_linear(act, w_out_p, layer, x, 0.5, 1024, 2816, emit_norm, "ffn_out")


def _mixer(x, xb, ss, batch, seq, layer, w_in_p, w_gate_b, w_a2_f, b_f, w_a2_b, b_b, out_norm,
           ln_g, ln_b, w_s, b_s, rpb, w_branch_b, w_out_b):
    proj = _normed_linear(xb, ss, w_in_p, layer, None, "mixer_in_proj")
    gates = _normed_linear(xb, ss, w_gate_b, layer, "sigmoid", "mixer_gates")

    w2f, bias_f = _pack_decay(w_a2_f, b_f, 0)
    w2b, bias_b = _pack_decay(w_a2_b, b_b, 1)
    o_gla = _gla(proj, w2f, w2b, bias_f, bias_b, out_norm, batch, seq)

    gmlp_bias = jnp.repeat(b_s.T, GMLP_GROUP_DIM, axis=1)
    o_gmlp = _gmlp(proj, ln_g, ln_b, w_s.astype(BF16), gmlp_bias)

    o_na = _na(proj, _na_bias_table(rpb, seq // GRID_W), batch, seq)

    y = _merge(o_gla, o_gmlp, o_na, w_branch_b, layer, gates)
    return _residual_linear(y, w_out_b, layer, x, 1.0, 512, 4096, True, "mixer_out")


def kernel(x, ffn1_norm, ffn1_w_in, ffn1_w_out, mix_norm, w_in, w_gate, gla_w_a2_fwd, gla_b_fwd, gla_w_a2_bwd, gla_b_bwd, gla_out_norm, gmlp_ln_g, gmlp_ln_b, gmlp_w_s, gmlp_b_s, na_rpb, w_branch, w_out, ffn2_norm, ffn2_w_in, ffn2_w_out, final_norm):
    batch, seq, d = x.shape
    depth = ffn1_norm.shape[0]
    assert seq % (NA_QROWS * GRID_W) == 0 and seq // GRID_W >= NA_KROWS and seq % GMLP_CHUNK == 0
    f_pad = _round_up(ffn1_w_out.shape[1], 1024)

    ffn1_in, ffn2_in = _scaled_bf16(ffn1_w_in, ffn1_norm), _scaled_bf16(ffn2_w_in, ffn2_norm)
    ffn1_out, ffn2_out = _pad_rows_bf16(ffn1_w_out, f_pad), _pad_rows_bf16(ffn2_w_out, f_pad)
    w_in_p = _pack_w_in(w_in, mix_norm)
    w_gate_b = _scaled_bf16(w_gate, mix_norm)
    w_branch_b, w_out_b = w_branch.astype(BF16), w_out.astype(BF16)

    xs = x.reshape(batch * seq, d)
    xb, ss = _prenorm(xs)
    for l in range(depth):
        xs, xb, ss = _ffn(xs, xb, ss, ffn1_in, ffn1_out, l, True)
        xs, xb, ss = _mixer(xs, xb, ss, batch, seq, l, w_in_p, w_gate_b, gla_w_a2_fwd[l], gla_b_fwd[l],
                            gla_w_a2_bwd[l], gla_b_bwd[l], gla_out_norm[l], gmlp_ln_g[l], gmlp_ln_b[l],
                            gmlp_w_s[l], gmlp_b_s[l], na_rpb[l], w_branch_b, w_out_b)
        last = l == depth - 1
        out = _ffn(xs, xb, ss, ffn2_in, ffn2_out, l, not last)
        xs, xb, ss = (out[0], None, None) if last else out
    return _rmsnorm(xs, final_norm, F32).reshape(batch, seq, d)
```

```python
import functools
import math

import numpy as np
import jax
import jax.numpy as jnp
from jax import lax
from jax.experimental import pallas as pl
from jax.experimental.pallas import tpu as pltpu

F32 = jnp.float32
BF16 = jnp.bfloat16
EPS = 1e-6

GRID_W = 64
GLA_HEADS, GLA_DK, GLA_DV, GLA_RANK, GLA_TAU, GLA_CHUNK = 4, 192, 384, 16, 16.0, 64
GLA_DKP = 256
GMLP_GROUPS, GMLP_GROUP_DIM, GMLP_CHUNK = 8, 128, 128
GMLP_WIDTH = GMLP_GROUPS * GMLP_GROUP_DIM
NA_HEADS, NA_HEAD_DIM, NA_WIN_ROWS, NA_WIN_COLS = 12, 128, 8, 16
NA_WIDTH = NA_HEADS * NA_HEAD_DIM
GLA_K_WIDTH = GLA_HEADS * GLA_DK
GLA_WIDTH = GLA_HEADS * GLA_DV
NA_QROWS = 4
NA_KROWS = NA_QROWS + NA_WIN_ROWS

OFF_GQ, OFF_GK, OFF_MU, OFF_GV, OFF_GG = 0, 1024, 2048, 3072, 4608
OFF_NQ, OFF_NK, OFF_NV, OFF_A, OFF_MV = 6144, 7680, 9216, 10752, 11264
PROJ_WIDTH = 12288
LANE = 128

VMEM_LIMIT = 56 * 1024 * 1024


def _round_up(x, m):
    return (x + m - 1) // m * m


def _tile(dim, pref):
    if dim <= pref:
        return dim
    t = pref - pref % LANE
    while dim % t:
        t -= LANE
    return t


def _params(*sem):
    return pltpu.CompilerParams(dimension_semantics=sem, vmem_limit_bytes=VMEM_LIMIT)


def _lane_group_sum(x):
    out = x[:, :LANE]
    for c in range(1, x.shape[1] // LANE):
        out = out + x[:, c * LANE:(c + 1) * LANE]
    return out


def _rstd(ss_ref, d):
    return lax.rsqrt(jnp.sum(ss_ref[...], axis=-1, keepdims=True) * (1.0 / d) + EPS)


def _prenorm_body(x_ref, xb_ref, ss_ref):
    x = x_ref[...]
    xb_ref[...] = x.astype(xb_ref.dtype)
    ss_ref[...] = _lane_group_sum(x * x)


def _prenorm(x):
    m, d = x.shape
    bm = _tile(m, 256)
    return pl.pallas_call(
        _prenorm_body,
        grid=(m // bm,),
        in_specs=[pl.BlockSpec((bm, d), lambda i: (i, 0))],
        out_specs=[pl.BlockSpec((bm, d), lambda i: (i, 0)),
                   pl.BlockSpec((bm, LANE), lambda i: (i, 0))],
        out_shape=[jax.ShapeDtypeStruct((m, d), BF16), jax.ShapeDtypeStruct((m, LANE), F32)],
        compiler_params=_params("arbitrary"),
        name="prenorm",
    )(x)


def _rmsnorm_body(x_ref, g_ref, o_ref):
    x = x_ref[...]
    ms = jnp.mean(x * x, axis=-1, keepdims=True)
    o_ref[...] = (x * lax.rsqrt(ms + EPS) * g_ref[...]).astype(o_ref.dtype)


def _rmsnorm(x, g, out_dtype):
    m, d = x.shape
    bm = _tile(m, 256)
    return pl.pallas_call(
        _rmsnorm_body,
        grid=(m // bm,),
        in_specs=[pl.BlockSpec((bm, d), lambda i: (i, 0)),
                  pl.BlockSpec((1, d), lambda i: (0, 0))],
        out_specs=pl.BlockSpec((bm, d), lambda i: (i, 0)),
        out_shape=jax.ShapeDtypeStruct((m, d), out_dtype),
        compiler_params=_params("arbitrary"),
        name="rmsnorm",
    )(x, g.reshape(1, d))


def _normed_linear_body(x_ref, ss_ref, w_ref, o_ref, *, act, w_transposed):
    contract = (((1,), (1 if w_transposed else 0,)), ((), ()))
    acc = _rstd(ss_ref, x_ref.shape[1]) * lax.dot_general(x_ref[...], w_ref[...], contract,
                                                          preferred_element_type=F32)
    if act == "sigmoid":
        acc = jax.nn.sigmoid(acc)
    o_ref[...] = acc.astype(o_ref.dtype)


def _normed_linear(xb, ss, w, layer, act, name, w_transposed=False):
    m, k = xb.shape
    n = w.shape[1] if w_transposed else w.shape[2]
    bm, bn = _tile(m, 1024), _tile(n, 1024)
    w_spec = (pl.BlockSpec((None, bn, k), lambda i, j: (layer, j, 0)) if w_transposed
              else pl.BlockSpec((None, k, bn), lambda i, j: (layer, 0, j)))
    return pl.pallas_call(
        functools.partial(_normed_linear_body, act=act, w_transposed=w_transposed),
        grid=(m // bm, n // bn),
        in_specs=[pl.BlockSpec((bm, k), lambda i, j: (i, 0)),
                  pl.BlockSpec((bm, ss.shape[1]), lambda i, j: (i, 0)),
                  w_spec],
        out_specs=pl.BlockSpec((bm, bn), lambda i, j: (i, j)),
        out_shape=jax.ShapeDtypeStruct((m, n), BF16),
        compiler_params=_params("arbitrary", "arbitrary"),
        name=name,
    )(xb, ss, w)


def _ffn_in_body(x_ref, ss_ref, wa_ref, wb0_ref, wb1_ref, o_ref, *, width):
    x = x_ref[...]
    rstd = _rstd(ss_ref, x_ref.shape[1])
    bn = o_ref.shape[1]
    half = bn // 2
    a = rstd * jnp.dot(x, wa_ref[...], preferred_element_type=F32)
    for c, wb_ref in enumerate((wb0_ref, wb1_ref)):
        cols = slice(c * half, (c + 1) * half)
        b = rstd * jnp.dot(x, wb_ref[...], preferred_element_type=F32)
        ac = a[:, cols]
        act = ac * jax.nn.sigmoid(ac) * b
        col = pl.program_id(1) * bn + c * half + lax.broadcasted_iota(jnp.int32, (1, half), 1)
        o_ref[:, cols] = jnp.where(col < width, act, 0.0).astype(o_ref.dtype)


def _ffn_in(xb, ss, w, layer, f_pad):
    m, k = xb.shape
    f = w.shape[2] // 2
    bm, bn = _tile(m, 1024), _tile(f_pad, 512)
    half = bn // 2
    assert f % half == 0
    nb = f // half
    last = 2 * nb - 1
    return pl.pallas_call(
        functools.partial(_ffn_in_body, width=f),
        grid=(m // bm, f_pad // bn),
        in_specs=[pl.BlockSpec((bm, k), lambda i, j: (i, 0)),
                  pl.BlockSpec((bm, ss.shape[1]), lambda i, j: (i, 0)),
                  pl.BlockSpec((None, k, bn), lambda i, j: (layer, 0, j)),
                  pl.BlockSpec((None, k, half), lambda i, j: (layer, 0, jnp.minimum(nb + 2 * j, last))),
                  pl.BlockSpec((None, k, half), lambda i, j: (layer, 0, jnp.minimum(nb + 2 * j + 1, last)))],
        out_specs=pl.BlockSpec((bm, bn), lambda i, j: (i, j)),
        out_shape=jax.ShapeDtypeStruct((m, f_pad), BF16),
        compiler_params=_params("arbitrary", "arbitrary"),
        name="ffn_in",
    )(xb, ss, w, w, w)


def _residual_body(a_ref, w_ref, x_ref, o_ref, *norm_refs):
    nk = pl.num_programs(2)

    @pl.when(pl.program_id(2) == 0)
    def _():
        o_ref[...] = x_ref[...]

    o_ref[...] += jnp.dot(a_ref[...], w_ref[...], preferred_element_type=F32)

    if norm_refs:
        xb_ref, ss_ref = norm_refs

        @pl.when(pl.program_id(2) == nk - 1)
        def _():
            xn = o_ref[...]
            xb_ref[...] = xn.astype(xb_ref.dtype)
            ss_ref[...] = _lane_group_sum(xn * xn)


def _residual_linear(a, w, layer, x, bn_pref, tk_pref, emit_norm, name):
    m, k = a.shape
    n = w.shape[2]
    bm, bn, tk = _tile(m, 1024), _tile(n, bn_pref), _tile(k, tk_pref)
    out_specs = [pl.BlockSpec((bm, bn), lambda i, j, kk: (i, j))]
    out_shape = [jax.ShapeDtypeStruct((m, n), F32)]
    if emit_norm:
        out_specs += [pl.BlockSpec((bm, bn), lambda i, j, kk: (i, j)),
                      pl.BlockSpec((bm, LANE), lambda i, j, kk: (i, j))]
        out_shape += [jax.ShapeDtypeStruct((m, n), BF16), jax.ShapeDtypeStruct((m, LANE * (n // bn)), F32)]
    return pl.pallas_call(
        _residual_body,
        grid=(m // bm, n // bn, k // tk),
        in_specs=[pl.BlockSpec((bm, tk), lambda i, j, kk: (i, kk)),
                  pl.BlockSpec((None, tk, bn), lambda i, j, kk: (layer, kk, j)),
                  pl.BlockSpec((bm, bn), lambda i, j, kk: (i, j))],
        out_specs=out_specs,
        out_shape=out_shape,
        compiler_params=_params("arbitrary", "arbitrary", "arbitrary"),
        name=name,
    )(a, w, x)


MERGE_ROWS = 512


def _merge_body(og_ref, om_ref, on_ref, *refs):
    nblk = (og_ref.shape[1] + om_ref.shape[1] + on_ref.shape[1]) // MERGE_ROWS
    w_refs, (g1_ref, g2_ref, g3_ref, y_ref) = refs[:nblk], refs[nblk:]
    y = None
    first = 0
    for o_ref, g_ref in ((og_ref, g1_ref), (om_ref, g2_ref), (on_ref, g3_ref)):
        cnt = o_ref.shape[1] // MERGE_ROWS
        w = jnp.concatenate([r[...] for r in w_refs[first:first + cnt]], axis=0)
        first += cnt
        term = g_ref[...].astype(F32) * jnp.dot(o_ref[...], w, preferred_element_type=F32)
        y = term if y is None else y + term
    y_ref[...] = y.astype(y_ref.dtype)


def _merge(o_gla, o_gmlp, o_na, w_branch, layer, gates):
    m = o_gla.shape[0]
    kk, d = w_branch.shape[1:]
    bm, bn = _tile(m, 1024), _tile(d, 512)
    nb = d // bn
    row = lambda i, j: (i, 0)
    w_specs = [pl.BlockSpec((None, MERGE_ROWS, bn), functools.partial(lambda i, j, r: (layer, r, j), r=r))
               for r in range(kk // MERGE_ROWS)]
    return pl.pallas_call(
        _merge_body,
        grid=(m // bm, nb),
        in_specs=[pl.BlockSpec((bm, o_gla.shape[1]), row),
                  pl.BlockSpec((bm, o_gmlp.shape[1]), row),
                  pl.BlockSpec((bm, o_na.shape[1]), row)] + w_specs +
                 [pl.BlockSpec((bm, bn), lambda i, j: (i, j)),
                  pl.BlockSpec((bm, bn), lambda i, j: (i, j + nb)),
                  pl.BlockSpec((bm, bn), lambda i, j: (i, j + 2 * nb))],
        out_specs=pl.BlockSpec((bm, bn), lambda i, j: (i, j)),
        out_shape=jax.ShapeDtypeStruct((m, d), BF16),
        compiler_params=_params("arbitrary", "arbitrary"),
        name="branch_merge",
    )(o_gla, o_gmlp, o_na, *([w_branch] * (kk // MERGE_ROWS)), gates, gates, gates)


GLA_ROWS = 256
GLA_HPS = 2


def _log_sigmoid(x):
    return jnp.minimum(x, 0.0) - jnp.log(1.0 + jnp.exp(-jnp.abs(x)))


def _dot_exact01(tri, x):
    hi = x.astype(BF16)
    rest = x - hi.astype(F32)
    mid = rest.astype(BF16)
    lo = (rest - mid.astype(F32)).astype(BF16)
    return (jnp.dot(tri, hi, preferred_element_type=F32) + jnp.dot(tri, mid, preferred_element_type=F32)
            + jnp.dot(tri, lo, preferred_element_type=F32))


def _gla_body(q_ref, k_ref, v_ref, g_ref, a_ref, w2f_ref, w2b_ref, bf_ref, bb_ref, gn_ref, o_ref,
              cf_s, cb_s, acc_s, sf_s, sb_s):
    seq = q_ref.shape[0]
    c = GLA_CHUNK
    kp, dv = GLA_DKP, GLA_DV
    n_chunks = seq // c
    rb = min(GLA_ROWS, seq)

    r = lax.broadcasted_iota(jnp.int32, (rb, rb), 0)
    cc = lax.broadcasted_iota(jnp.int32, (rb, rb), 1)
    same = lax.shift_right_logical(r, 6) == lax.shift_right_logical(cc, 6)
    tri_f = jnp.where(same & (cc <= r), 1.0, 0.0).astype(BF16)
    tri_b = jnp.where(same & (cc >= r), 1.0, 0.0).astype(BF16)

    def decay_step(i, carry):
        rows = pl.ds(pl.multiple_of(i * rb, rb), rb)
        a = a_ref[rows, :]
        la_f = _log_sigmoid(jnp.dot(a, w2f_ref[0], preferred_element_type=F32) + bf_ref[0]) * (1.0 / GLA_TAU)
        la_b = _log_sigmoid(jnp.dot(a, w2b_ref[0], preferred_element_type=F32) + bb_ref[0]) * (1.0 / GLA_TAU)
        cf_s[rows, :] = _dot_exact01(tri_f, la_f)
        cb_s[rows, :] = _dot_exact01(tri_b, la_b)
        acc_s[rows, :] = jnp.zeros((rb, acc_s.shape[1]), F32)
        return carry

    lax.fori_loop(0, seq // rb, decay_step, 0)
    sf_s[...] = jnp.zeros(sf_s.shape, F32)
    sb_s[...] = jnp.zeros(sb_s.shape, F32)

    ii = lax.broadcasted_iota(jnp.int32, (c, c), 0)
    jj = lax.broadcasted_iota(jnp.int32, (c, c), 1)
    mask_f = jj <= ii
    mask_b = jj > ii
    q_scale = GLA_DK ** -0.5

    def chunk(idx, hh, cum_s, state_s, mask, mid, last):
        rows = pl.ds(pl.multiple_of(idx * c, c), c)
        kcols = slice(hh * kp, (hh + 1) * kp)
        vcols = slice(hh * dv, (hh + 1) * dv)
        cum = cum_s[rows, kcols]
        b_mid = cum[mid:mid + 1, :]
        b_last = cum[last:last + 1, :]
        q = q_ref[rows, kcols].astype(F32) * q_scale
        k = k_ref[rows, kcols].astype(F32)
        v = v_ref[rows, vcols]
        q_intra = (q * jnp.exp(cum - b_mid)).astype(BF16)
        k_intra = (k * jnp.exp(b_mid - cum)).astype(BF16)
        scores = lax.dot_general(q_intra, k_intra, (((1,), (1,)), ((), ())), preferred_element_type=F32)
        scores = jnp.where(mask, scores, 0.0).astype(BF16)
        o = jnp.dot(scores, v, preferred_element_type=F32)
        state = state_s[hh]
        q_inter = (q * jnp.exp(cum)).astype(BF16)
        o += lax.dot_general(q_inter, state.astype(BF16), (((1,), (1,)), ((), ())), preferred_element_type=F32)
        k_state = (k * jnp.exp(b_last - cum)).astype(BF16)
        kv = lax.dot_general(v, k_state, (((0,), (0,)), ((), ())), preferred_element_type=F32)
        state_s[hh] = state * jnp.exp(b_last) + kv
        acc_s[rows, vcols] += o

    def scan_step(n, carry):
        for hh in range(sf_s.shape[0]):
            chunk(n, hh, cf_s, sf_s, mask_f, c // 2 - 1, c - 1)
            chunk(n_chunks - 1 - n, hh, cb_s, sb_s, mask_b, c // 2, 0)
        return carry

    lax.fori_loop(0, n_chunks, scan_step, 0, unroll=4)

    def out_step(i, carry):
        rows = pl.ds(pl.multiple_of(i * rb, rb), rb)
        for hh in range(sf_s.shape[0]):
            vcols = slice(hh * dv, (hh + 1) * dv)
            o = acc_s[rows, vcols]
            ms = jnp.mean(o * o, axis=-1, keepdims=True)
            g = g_ref[rows, vcols].astype(F32)
            o_ref[rows, vcols] = (o * lax.rsqrt(ms + EPS) * gn_ref[...] * (g * jax.nn.sigmoid(g))).astype(o_ref.dtype)
        return carry

    lax.fori_loop(0, seq // rb, out_step, 0)


def _gla(proj, w2f, w2b, bias_f, bias_b, out_norm, batch, seq):
    m = proj.shape[0]
    hps = GLA_HPS
    kb, vb = hps * GLA_DKP, hps * GLA_DV
    assert OFF_GQ % kb == 0 and OFF_GK % kb == 0 and OFF_GV % vb == 0 and OFF_GG % vb == 0
    return pl.pallas_call(
        _gla_body,
        grid=(batch, GLA_HEADS // hps),
        in_specs=[pl.BlockSpec((seq, kb), lambda b, h: (b, OFF_GQ // kb + h)),
                  pl.BlockSpec((seq, kb), lambda b, h: (b, OFF_GK // kb + h)),
                  pl.BlockSpec((seq, vb), lambda b, h: (b, OFF_GV // vb + h)),
                  pl.BlockSpec((seq, vb), lambda b, h: (b, OFF_GG // vb + h)),
                  pl.BlockSpec((seq, LANE), lambda b, h: (b, OFF_A // LANE)),
                  pl.BlockSpec((1, LANE, kb), lambda b, h: (h, 0, 0)),
                  pl.BlockSpec((1, LANE, kb), lambda b, h: (h, 0, 0)),
                  pl.BlockSpec((1, 1, kb), lambda b, h: (h, 0, 0)),
                  pl.BlockSpec((1, 1, kb), lambda b, h: (h, 0, 0)),
                  pl.BlockSpec((1, GLA_DV), lambda b, h: (0, 0))],
        out_specs=pl.BlockSpec((seq, vb), lambda b, h: (b, h)),
        out_shape=jax.ShapeDtypeStruct((m, GLA_WIDTH), BF16),
        scratch_shapes=[pltpu.VMEM((seq, kb), F32), pltpu.VMEM((seq, kb), F32),
                        pltpu.VMEM((seq, vb), F32),
                        pltpu.VMEM((hps, GLA_DV, GLA_DKP), F32), pltpu.VMEM((hps, GLA_DV, GLA_DKP), F32)],
        compiler_params=_params("arbitrary", "arbitrary"),
        name="gla_mixer",
    )(proj, proj, proj, proj, proj, w2f, w2b, bias_f, bias_b, out_norm.reshape(1, GLA_DV))


GMLP_ROWS = 512


def _gelu_tanh(x):
    return 0.5 * x * (1.0 + jnp.tanh(math.sqrt(2.0 / math.pi) * (x + 0.044715 * (x * x * x))))


def _gmlp_body(u_ref, v_ref, lg_ref, lb_ref, ws_ref, bs_ref, o_ref):
    t = GMLP_CHUNK
    for ci in range(u_ref.shape[0] // t):
        rows = slice(ci * t, (ci + 1) * t)
        vf = _gelu_tanh(v_ref[rows, :].astype(F32))
        mu = jnp.mean(vf, axis=-1, keepdims=True)
        var = jnp.mean(jnp.square(vf - mu), axis=-1, keepdims=True)
        vn = ((vf - mu) * lax.rsqrt(var + EPS) * lg_ref[...] + lb_ref[...]).astype(BF16)
        for g in range(GMLP_GROUPS):
            cols = slice(g * GMLP_GROUP_DIM, (g + 1) * GMLP_GROUP_DIM)
            s = jnp.dot(ws_ref[g], vn[:, cols], preferred_element_type=F32) + bs_ref[:, cols]
            u = _gelu_tanh(u_ref[rows, cols].astype(F32))
            o_ref[rows, cols] = (u * s).astype(o_ref.dtype)


def _gmlp(proj, ln_g, ln_b, w_s, bias):
    m = proj.shape[0]
    w = GMLP_WIDTH
    bm = _tile(m, GMLP_ROWS)
    return pl.pallas_call(
        _gmlp_body,
        grid=(m // bm,),
        in_specs=[pl.BlockSpec((bm, w), lambda i: (i, OFF_MU // w)),
                  pl.BlockSpec((bm, w), lambda i: (i, OFF_MV // w)),
                  pl.BlockSpec((1, w), lambda i: (0, 0)),
                  pl.BlockSpec((1, w), lambda i: (0, 0)),
                  pl.BlockSpec((GMLP_GROUPS, GMLP_CHUNK, GMLP_CHUNK), lambda i: (0, 0, 0)),
                  pl.BlockSpec((GMLP_CHUNK, w), lambda i: (0, 0))],
        out_specs=pl.BlockSpec((bm, w), lambda i: (i, 0)),
        out_shape=jax.ShapeDtypeStruct((m, w), BF16),
        compiler_params=_params("arbitrary"),
        name="gmlp_mixer",
    )(proj, proj, ln_g.reshape(1, w), ln_b.reshape(1, w), w_s, bias)


def _na_key_start(qb, rows):
    return int(np.clip(qb * NA_QROWS - NA_WIN_ROWS // 2, 0, rows - NA_KROWS))


def _na_table_index(qb, nqb):
    return 0 if qb == 0 else (2 if qb == nqb - 1 else 1)


def _na_body(q_ref, k_ref, v_ref, bias_ref, o_ref, *, rows):
    nqb = rows // NA_QROWS
    qn = NA_QROWS * GRID_W
    for qb in range(nqb):
        start = _na_key_start(qb, rows) * GRID_W
        keys = slice(start, start + NA_KROWS * GRID_W)
        qrows = slice(qb * qn, (qb + 1) * qn)
        s = lax.dot_general(q_ref[qrows, :], k_ref[keys, :], (((1,), (1,)), ((), ())),
                            preferred_element_type=F32)
        s = s * (NA_HEAD_DIM ** -0.5) + bias_ref[0, _na_table_index(qb, nqb)]
        p = jnp.exp(s - jnp.max(s, axis=-1, keepdims=True))
        l = jnp.sum(p, axis=-1, keepdims=True)
        o = jnp.dot(p.astype(BF16), v_ref[keys, :], preferred_element_type=F32)
        o_ref[qrows, :] = (o / l).astype(o_ref.dtype)


def _na_toeplitz(rpb):
    h, ny, nx = rpb.shape
    w = GRID_W
    left = w - NA_WIN_COLS
    v = jnp.pad(rpb, ((0, 0), (0, 0), (left, 2 * w - left - nx)))
    flat = jnp.broadcast_to(v[:, :, None, :], (h, ny, w, 2 * w)).reshape(h, ny, 2 * w * w)
    skew = flat[:, :, : w * (2 * w - 1)].reshape(h, ny, w, 2 * w - 1)
    return skew[..., w - 1:]


def _na_bias_table(rpb, rows):
    nqb = rows // NA_QROWS
    ny = 2 * NA_WIN_ROWS - 1
    toe = jnp.pad(_na_toeplitz(rpb), ((0, 0), (NA_KROWS, NA_KROWS), (0, 0), (0, 0)))
    qc = np.arange(GRID_W)[:, None]
    kc = np.arange(GRID_W)[None, :]
    c0 = np.clip(qc - NA_WIN_COLS // 2, 0, GRID_W - NA_WIN_COLS)
    col_ok = (kc >= c0) & (kc < c0 + NA_WIN_COLS)
    tables = []
    for qb in (0, 1, nqb - 1):
        start = _na_key_start(qb, rows)
        blocks, row_ok = [], []
        for a in range(NA_QROWS):
            qr = qb * NA_QROWS + a
            r0 = int(np.clip(qr - NA_WIN_ROWS // 2, 0, rows - NA_WIN_ROWS))
            kr = start + np.arange(NA_KROWS)
            row_ok.append((kr >= r0) & (kr < r0 + NA_WIN_ROWS))
            dy0 = start - qr + NA_WIN_ROWS - 1
            assert -NA_KROWS <= dy0 <= ny
            blocks.append(lax.slice_in_dim(toe, NA_KROWS + dy0, 2 * NA_KROWS + dy0, axis=1))
        blk = jnp.stack(blocks, axis=1)
        blk = blk.transpose(0, 1, 3, 2, 4)
        ok = np.stack(row_ok)[:, None, :, None] & col_ok[None, :, None, :]
        tab = jnp.where(ok[None], blk, -jnp.inf)
        tables.append(tab.reshape(rpb.shape[0], NA_QROWS * GRID_W, NA_KROWS * GRID_W))
    return jnp.stack(tables, axis=1).astype(F32)


def _na(proj, bias_table, batch, seq):
    m = proj.shape[0]
    rows = seq // GRID_W
    qn = NA_QROWS * GRID_W
    kn = NA_KROWS * GRID_W
    d = NA_HEAD_DIM
    return pl.pallas_call(
        functools.partial(_na_body, rows=rows),
        grid=(NA_HEADS, batch),
        in_specs=[pl.BlockSpec((seq, d), lambda h, b: (b, OFF_NQ // d + h)),
                  pl.BlockSpec((seq, d), lambda h, b: (b, OFF_NK // d + h)),
                  pl.BlockSpec((seq, d), lambda h, b: (b, OFF_NV // d + h)),
                  pl.BlockSpec((1, 3, qn, kn), lambda h, b: (h, 0, 0, 0))],
        out_specs=pl.BlockSpec((seq, d), lambda h, b: (b, h)),
        out_shape=jax.ShapeDtypeStruct((m, NA_WIDTH), BF16),
        compiler_params=_params("arbitrary", "arbitrary"),
        name="na_mixer",
    )(proj, proj, proj, bias_table)


def _pack_moves():
    src_off = np.concatenate([[0], np.cumsum((GLA_K_WIDTH, GLA_K_WIDTH, GLA_WIDTH, GLA_WIDTH, GLA_RANK, GLA_RANK,
                                              GMLP_WIDTH, GMLP_WIDTH, NA_WIDTH, NA_WIDTH, NA_WIDTH))]).tolist()
    s_gq, s_gk, s_gv, s_gg, s_af, s_ab, s_mu, s_mv, s_nq, s_nk, s_nv, _ = src_off
    moves = []
    for h in range(GLA_HEADS):
        moves.append((OFF_GQ + h * GLA_DKP, s_gq + h * GLA_DK, GLA_DK))
        moves.append((OFF_GK + h * GLA_DKP, s_gk + h * GLA_DK, GLA_DK))
    moves += [(OFF_MU, s_mu, GMLP_WIDTH), (OFF_GV, s_gv, GLA_WIDTH), (OFF_GG, s_gg, GLA_WIDTH),
              (OFF_NQ, s_nq, NA_WIDTH), (OFF_NK, s_nk, NA_WIDTH), (OFF_NV, s_nv, NA_WIDTH),
              (OFF_A, s_af, 2 * GLA_RANK), (OFF_MV, s_mv, GMLP_WIDTH)]
    return sorted(moves)


def _pack_in_body(w_ref, g_ref, o_ref):
    g = g_ref[...]
    cursor = 0
    for dst, src, width in _pack_moves():
        if dst > cursor:
            o_ref[cursor:dst, :] = jnp.zeros((dst - cursor, o_ref.shape[1]), o_ref.dtype)
        o_ref[dst:dst + width, :] = (w_ref[src:src + width, :] * g).astype(o_ref.dtype)
        cursor = dst + width
    if cursor < o_ref.shape[0]:
        o_ref[cursor:, :] = jnp.zeros((o_ref.shape[0] - cursor, o_ref.shape[1]), o_ref.dtype)


def _pack_w_in(w, g):
    depth, d, n = w.shape
    wt = jnp.swapaxes(w, 1, 2)
    bc = _tile(d, 256)
    return pl.pallas_call(
        _pack_in_body,
        grid=(depth, d // bc),
        in_specs=[pl.BlockSpec((None, n, bc), lambda l, i: (l, 0, i)),
                  pl.BlockSpec((None, 1, bc), lambda l, i: (l, 0, i))],
        out_specs=pl.BlockSpec((None, PROJ_WIDTH, bc), lambda l, i: (l, 0, i)),
        out_shape=jax.ShapeDtypeStruct((depth, PROJ_WIDTH, d), BF16),
        compiler_params=_params("arbitrary", "arbitrary"),
        name="pack_w_in",
    )(wt, g[:, None, :])


def _pack_decay(w2, bias, slot):
    groups = GLA_HEADS // GLA_HPS
    w = jnp.pad(w2.reshape(GLA_RANK, GLA_HEADS, GLA_DK), ((0, 0), (0, 0), (0, GLA_DKP - GLA_DK)))
    w = w.reshape(GLA_RANK, groups, GLA_HPS * GLA_DKP).transpose(1, 0, 2)
    w = jnp.pad(w, ((0, 0), (slot * GLA_RANK, LANE - (slot + 1) * GLA_RANK), (0, 0)))
    b = jnp.pad(bias.reshape(GLA_HEADS, GLA_DK), ((0, 0), (0, GLA_DKP - GLA_DK)))
    return w.astype(BF16), b.reshape(groups, 1, GLA_HPS * GLA_DKP).astype(F32)


def _scaled_bf16(w, g):
    return (w * g[:, :, None]).astype(BF16)


def _cast_pad_body(w_ref, o_ref, *, valid_rows, scale):
    br = o_ref.shape[0]
    row = pl.program_id(1) * br + lax.broadcasted_iota(jnp.int32, (br, 1), 0)
    o_ref[...] = jnp.where(row < valid_rows, w_ref[...] * scale, 0.0).astype(o_ref.dtype)


def _cast_pad_rows(w, rows, scale):
    depth, k, n = w.shape
    br = _tile(rows, 512)
    return pl.pallas_call(
        functools.partial(_cast_pad_body, valid_rows=k, scale=scale),
        grid=(depth, rows // br),
        in_specs=[pl.BlockSpec((None, br, n), lambda l, i: (l, i, 0))],
        out_specs=pl.BlockSpec((None, br, n), lambda l, i: (l, i, 0)),
        out_shape=jax.ShapeDtypeStruct((depth, rows, n), BF16),
        compiler_params=_params("arbitrary", "arbitrary"),
        name="cast_pad_rows",
    )(w)


def _ffn(x, xb, ss, w_in, w_out_p, layer, emit_norm):
    act = _ffn_in(xb, ss, w_in, layer, w_out_p.shape[1])
    return _residual_linear(act, w_out_p, layer, x, 1024, 2816, emit_norm, "ffn_out")


def _mixer(x, xb, ss, batch, seq, layer, w_in_p, w_gate_b, w_a2_f, b_f, w_a2_b, b_b, out_norm,
           ln_g, ln_b, w_s, b_s, rpb, w_branch_b, w_out_b):
    proj = _normed_linear(xb, ss, w_in_p, layer, None, "mixer_in_proj", w_transposed=True)
    gates = _normed_linear(xb, ss, w_gate_b, layer, "sigmoid", "mixer_gates")

    w2f, bias_f = _pack_decay(w_a2_f, b_f, 0)
    w2b, bias_b = _pack_decay(w_a2_b, b_b, 1)
    o_gla = _gla(proj, w2f, w2b, bias_f, bias_b, out_norm, batch, seq)

    gmlp_bias = jnp.repeat(b_s.T, GMLP_GROUP_DIM, axis=1)
    o_gmlp = _gmlp(proj, ln_g, ln_b, w_s.astype(BF16), gmlp_bias)

    o_na = _na(proj, _na_bias_table(rpb, seq // GRID_W), batch, seq)

    y = _merge(o_gla, o_gmlp, o_na, w_branch_b, layer, gates)
    return _residual_linear(y, w_out_b, layer, x, 512, 4096, True, "mixer_out")


def kernel(x, ffn1_norm, ffn1_w_in, ffn1_w_out, mix_norm, w_in, w_gate, gla_w_a2_fwd, gla_b_fwd, gla_w_a2_bwd, gla_b_bwd, gla_out_norm, gmlp_ln_g, gmlp_ln_b, gmlp_w_s, gmlp_b_s, na_rpb, w_branch, w_out, ffn2_norm, ffn2_w_in, ffn2_w_out, final_norm):
    batch, seq, d = x.shape
    depth = ffn1_norm.shape[0]
    assert seq % (NA_QROWS * GRID_W) == 0 and seq // GRID_W >= NA_KROWS and seq % GMLP_CHUNK == 0
    f_pad = _round_up(ffn1_w_out.shape[1], 1024)

    ffn1_in, ffn2_in = _scaled_bf16(ffn1_w_in, ffn1_norm), _scaled_bf16(ffn2_w_in, ffn2_norm)
    ffn1_out, ffn2_out = _cast_pad_rows(ffn1_w_out, f_pad, 0.5), _cast_pad_rows(ffn2_w_out, f_pad, 0.5)
    w_in_p = _pack_w_in(w_in, mix_norm)
    w_gate_b = _scaled_bf16(w_gate, mix_norm)
    w_branch_b, w_out_b = w_branch.astype(BF16), w_out.astype(BF16)

    xs = x.reshape(batch * seq, d)
    xb, ss = _prenorm(xs)
    for l in range(depth):
        xs, xb, ss = _ffn(xs, xb, ss, ffn1_in, ffn1_out, l, True)
        xs, xb, ss = _mixer(xs, xb, ss, batch, seq, l, w_in_p, w_gate_b, gla_w_a2_fwd[l], gla_b_fwd[l],
                            gla_w_a2_bwd[l], gla_b_bwd[l], gla_out_norm[l], gmlp_ln_g[l], gmlp_ln_b[l],
                            gmlp_w_s[l], gmlp_b_s[l], na_rpb[l], w_branch_b, w_out_b)
        last = l == depth - 1
        out = _ffn(xs, xb, ss, ffn2_in, ffn2_out, l, not last)
        xs, xb, ss = (out[0], None, None) if last else out
    return _rmsnorm(xs, final_norm, F32).reshape(batch, seq, d)
```

```python
import functools
import math
import typing

import numpy as np
import jax
import jax.numpy as jnp
from jax import lax
from jax.experimental import pallas as pl
from jax.experimental.pallas import tpu as pltpu

F32 = jnp.float32
BF16 = jnp.bfloat16
EPS = 1e-6

GRID_W = 64
GLA_HEADS, GLA_DK, GLA_DV, GLA_RANK, GLA_TAU, GLA_CHUNK = 4, 192, 384, 16, 16.0, 64
GLA_DKP = 256
GMLP_GROUPS, GMLP_GROUP_DIM, GMLP_CHUNK = 8, 128, 128
GMLP_WIDTH = GMLP_GROUPS * GMLP_GROUP_DIM
NA_HEADS, NA_HEAD_DIM, NA_WIN_ROWS, NA_WIN_COLS = 12, 128, 8, 16
NA_WIDTH = NA_HEADS * NA_HEAD_DIM
GLA_K_WIDTH = GLA_HEADS * GLA_DK
GLA_WIDTH = GLA_HEADS * GLA_DV
NA_QROWS = 4
NA_KROWS = NA_QROWS + NA_WIN_ROWS

OFF_GQ, OFF_GK, OFF_MU, OFF_GV, OFF_GG = 0, 1024, 2048, 3072, 4608
OFF_NQ, OFF_NK, OFF_NV, OFF_A, OFF_MV = 6144, 7680, 9216, 10752, 11264
PROJ_WIDTH = 12288
LANE = 128

VMEM_LIMIT = 56 * 1024 * 1024


def _round_up(x, m):
    return (x + m - 1) // m * m


def _tile(dim, pref):
    if dim <= pref:
        return dim
    t = pref - pref % LANE
    while dim % t:
        t -= LANE
    return t


def _params(*sem):
    return pltpu.CompilerParams(dimension_semantics=sem, vmem_limit_bytes=VMEM_LIMIT)


def _lane_group_sum(x):
    out = x[:, :LANE]
    for c in range(1, x.shape[1] // LANE):
        out = out + x[:, c * LANE:(c + 1) * LANE]
    return out


def _rstd(ss_ref, d):
    return lax.rsqrt(jnp.sum(ss_ref[...], axis=-1, keepdims=True) * (1.0 / d) + EPS)


def _prenorm_body(x_ref, xb_ref, ss_ref):
    x = x_ref[...]
    xb_ref[...] = x.astype(xb_ref.dtype)
    ss_ref[...] = _lane_group_sum(x * x)


def _prenorm(x):
    m, d = x.shape
    bm = _tile(m, 256)
    return pl.pallas_call(
        _prenorm_body,
        grid=(m // bm,),
        in_specs=[pl.BlockSpec((bm, d), lambda i: (i, 0))],
        out_specs=[pl.BlockSpec((bm, d), lambda i: (i, 0)),
                   pl.BlockSpec((bm, LANE), lambda i: (i, 0))],
        out_shape=[jax.ShapeDtypeStruct((m, d), BF16), jax.ShapeDtypeStruct((m, LANE), F32)],
        compiler_params=_params("arbitrary"),
        name="prenorm",
    )(x)


def _rmsnorm_body(x_ref, g_ref, o_ref):
    x = x_ref[...]
    ms = jnp.mean(x * x, axis=-1, keepdims=True)
    o_ref[...] = (x * lax.rsqrt(ms + EPS) * g_ref[...]).astype(o_ref.dtype)


def _rmsnorm(x, g, out_dtype):
    m, d = x.shape
    bm = _tile(m, 256)
    return pl.pallas_call(
        _rmsnorm_body,
        grid=(m // bm,),
        in_specs=[pl.BlockSpec((bm, d), lambda i: (i, 0)),
                  pl.BlockSpec((1, d), lambda i: (0, 0))],
        out_specs=pl.BlockSpec((bm, d), lambda i: (i, 0)),
        out_shape=jax.ShapeDtypeStruct((m, d), out_dtype),
        compiler_params=_params("arbitrary"),
        name="rmsnorm",
    )(x, g.reshape(1, d))


def _normed_linear_body(x_ref, ss_ref, w_ref, o_ref, *, act, w_transposed):
    contract = (((1,), (1 if w_transposed else 0,)), ((), ()))
    acc = _rstd(ss_ref, x_ref.shape[1]) * lax.dot_general(x_ref[...], w_ref[...], contract,
                                                          preferred_element_type=F32)
    if act == "sigmoid":
        acc = _sigmoid(acc)
    o_ref[...] = acc.astype(o_ref.dtype)


def _normed_linear(xb, ss, w, layer, act, name, w_transposed=False):
    m, k = xb.shape
    n = w.shape[1] if w_transposed else w.shape[2]
    bm, bn = _tile(m, 1024), _tile(n, 1024)
    w_spec = (pl.BlockSpec((None, bn, k), lambda i, j: (layer, j, 0)) if w_transposed
              else pl.BlockSpec((None, k, bn), lambda i, j: (layer, 0, j)))
    return pl.pallas_call(
        functools.partial(_normed_linear_body, act=act, w_transposed=w_transposed),
        grid=(m // bm, n // bn),
        in_specs=[pl.BlockSpec((bm, k), lambda i, j: (i, 0)),
                  pl.BlockSpec((bm, ss.shape[1]), lambda i, j: (i, 0)),
                  w_spec],
        out_specs=pl.BlockSpec((bm, bn), lambda i, j: (i, j)),
        out_shape=jax.ShapeDtypeStruct((m, n), BF16),
        compiler_params=_params("arbitrary", "arbitrary"),
        name=name,
    )(xb, ss, w)


RIDER_BLOCK_BYTES = 3 * 1024 * 1024


class _Rider(typing.NamedTuple):
    src: jax.Array
    layer: int
    gain: typing.Optional[jax.Array]
    scale: float
    out_rows: int


def _rider_rows(rider, steps):
    _, r, c = rider.src.shape
    for rr in range(16, min(RIDER_BLOCK_BYTES // (4 * c), r) + 1, 16):
        if r % rr == 0 and rider.out_rows % rr == 0 and rider.out_rows // rr <= steps:
            return rr
    return None


def _rider_specs(rider, rr, step_of):
    _, r, c = rider.src.shape
    n_valid, n_out = r // rr, rider.out_rows // rr
    dst_idx = lambda *g: jnp.minimum(step_of(*g), n_out - 1)
    src_map = lambda *g: (rider.layer, jnp.minimum(dst_idx(*g), n_valid - 1), 0)
    in_specs, in_args = [pl.BlockSpec((None, rr, c), src_map)], [rider.src]
    if rider.gain is not None:
        in_specs.append(pl.BlockSpec((None, rr, 1), src_map))
        in_args.append(rider.gain)
    out_spec = pl.BlockSpec((rr, c), lambda *g: (dst_idx(*g), 0))
    return in_specs, in_args, out_spec, jax.ShapeDtypeStruct((rider.out_rows, c), BF16)


def _rider_step(refs, step, n_valid, n_out, scale):
    src_ref, dst_ref = refs[0], refs[-1]
    v = src_ref[...]
    if len(refs) == 3:
        v = v * refs[1][...]
    if scale != 1.0:
        v = v * scale
    if n_out > n_valid:
        v = jnp.where(jnp.minimum(step, n_out - 1) < n_valid, v, 0.0)
    dst_ref[...] = v.astype(dst_ref.dtype)


def _sigmoid(x):
    return 0.5 * jnp.tanh(0.5 * x) + 0.5


def _ffn_in_body(x_ref, ss_ref, wa_ref, wb0_ref, wb1_ref, *refs, width, tail, riders):
    n_rider_in = sum(meta[0] for meta in riders)
    o_ref = refs[n_rider_in]
    bn = o_ref.shape[1]
    half = bn // 2

    def ride():
        step = pl.program_id(0) * pl.num_programs(1) + pl.program_id(1)
        first = 0
        for r, meta in enumerate(riders):
            _rider_step(refs[first:first + meta[0]] + (refs[n_rider_in + 1 + r],), step, *meta[1:])
            first += meta[0]

    def compute(valid):
        ride()
        x = x_ref[...]
        rstd = _rstd(ss_ref, x_ref.shape[1])
        a = rstd * jnp.dot(x, wa_ref[:, :valid], preferred_element_type=F32)
        for c, wb_ref in enumerate((wb0_ref, wb1_ref)[:valid // half]):
            cols = slice(c * half, (c + 1) * half)
            b = rstd * jnp.dot(x, wb_ref[...], preferred_element_type=F32)
            ac = a[:, cols]
            o_ref[:, cols] = (ac * _sigmoid(ac) * b).astype(o_ref.dtype)
        if valid < bn:
            o_ref[:, valid:] = jnp.zeros((o_ref.shape[0], bn - valid), o_ref.dtype)

    if tail == bn:
        compute(bn)
    else:
        is_tail = (pl.program_id(1) + 1) * bn > width
        pl.when(jnp.logical_not(is_tail))(lambda: compute(bn))
        pl.when(is_tail)(lambda: compute(tail))


def _ffn_in(xb, ss, w, layer, f_pad, riders=()):
    m, k = xb.shape
    f = w.shape[2] // 2
    bm, bn = _tile(m, 1024), _tile(f_pad, 512)
    half = bn // 2
    nb = f // half
    last = 2 * nb - 1
    tail = f - (f_pad // bn - 1) * bn
    assert f % half == 0 and tail in (half, bn)
    grid = (m // bm, f_pad // bn)
    in_specs = [pl.BlockSpec((bm, k), lambda i, j: (i, 0)),
                pl.BlockSpec((bm, ss.shape[1]), lambda i, j: (i, 0)),
                pl.BlockSpec((None, k, bn), lambda i, j: (layer, 0, j)),
                pl.BlockSpec((None, k, half), lambda i, j: (layer, 0, jnp.minimum(nb + 2 * j, last))),
                pl.BlockSpec((None, k, half), lambda i, j: (layer, 0, jnp.minimum(nb + 2 * j + 1, last)))]
    args = [xb, ss, w, w, w]
    out_specs = [pl.BlockSpec((bm, bn), lambda i, j: (i, j))]
    out_shape = [jax.ShapeDtypeStruct((m, f_pad), BF16)]
    metas, hosted = [], []
    for rider in riders:
        rr = _rider_rows(rider, grid[0] * grid[1])
        hosted.append(rr is not None)
        if rr is None:
            continue
        r_in, r_args, r_out, r_shape = _rider_specs(rider, rr, lambda i, j: i * grid[1] + j)
        in_specs += r_in
        args += r_args
        out_specs.append(r_out)
        out_shape.append(r_shape)
        metas.append((len(r_in), rider.src.shape[1] // rr, rider.out_rows // rr, rider.scale))
    out = pl.pallas_call(
        functools.partial(_ffn_in_body, width=f, tail=tail, riders=tuple(metas)),
        grid=grid,
        in_specs=in_specs,
        out_specs=out_specs,
        out_shape=out_shape,
        compiler_params=_params("arbitrary", "arbitrary"),
        name="ffn_in",
    )(*args)
    converted = iter(out[1:])
    return out[0], [next(converted) if h else None for h in hosted]


def _residual_body(a_ref, w_ref, x_ref, o_ref, *norm_refs):
    nk = pl.num_programs(2)

    @pl.when(pl.program_id(2) == 0)
    def _():
        o_ref[...] = x_ref[...]

    o_ref[...] += jnp.dot(a_ref[...], w_ref[...], preferred_element_type=F32)

    if norm_refs:
        xb_ref, ss_ref = norm_refs

        @pl.when(pl.program_id(2) == nk - 1)
        def _():
            xn = o_ref[...]
            xb_ref[...] = xn.astype(xb_ref.dtype)
            part = _lane_group_sum(xn * xn)

            @pl.when(pl.program_id(1) == 0)
            def _():
                ss_ref[...] = part

            @pl.when(pl.program_id(1) > 0)
            def _():
                ss_ref[...] += part


def _residual_linear(a, w, layer, x, bn_pref, tk_pref, emit_norm, name):
    m, k = a.shape
    n = w.shape[2]
    bm, bn, tk = _tile(m, 1024), _tile(n, bn_pref), _tile(k, tk_pref)
    out_specs = [pl.BlockSpec((bm, bn), lambda i, j, kk: (i, j))]
    out_shape = [jax.ShapeDtypeStruct((m, n), F32)]
    if emit_norm:
        out_specs += [pl.BlockSpec((bm, bn), lambda i, j, kk: (i, j)),
                      pl.BlockSpec((bm, LANE), lambda i, j, kk: (i, 0))]
        out_shape += [jax.ShapeDtypeStruct((m, n), BF16), jax.ShapeDtypeStruct((m, LANE), F32)]
    return pl.pallas_call(
        _residual_body,
        grid=(m // bm, n // bn, k // tk),
        in_specs=[pl.BlockSpec((bm, tk), lambda i, j, kk: (i, kk)),
                  pl.BlockSpec((None, tk, bn), lambda i, j, kk: (layer, kk, j)),
                  pl.BlockSpec((bm, bn), lambda i, j, kk: (i, j))],
        out_specs=out_specs,
        out_shape=out_shape,
        compiler_params=_params("arbitrary", "arbitrary", "arbitrary"),
        name=name,
    )(a, w, x)


MERGE_ROWS = 512


def _merge_body(og_ref, om_ref, on_ref, *refs):
    nblk = (og_ref.shape[1] + om_ref.shape[1] + on_ref.shape[1]) // MERGE_ROWS
    w_refs, (g1_ref, g2_ref, g3_ref, y_ref) = refs[:nblk], refs[nblk:]
    y = None
    first = 0
    for o_ref, g_ref in ((og_ref, g1_ref), (om_ref, g2_ref), (on_ref, g3_ref)):
        cnt = o_ref.shape[1] // MERGE_ROWS
        w = jnp.concatenate([r[...] for r in w_refs[first:first + cnt]], axis=0)
        first += cnt
        term = g_ref[...].astype(F32) * jnp.dot(o_ref[...], w, preferred_element_type=F32)
        y = term if y is None else y + term
    y_ref[...] = y.astype(y_ref.dtype)


def _merge(o_gla, o_gmlp, o_na, w_branch, layer, gates):
    m = o_gla.shape[0]
    kk, d = w_branch.shape[1:]
    bm, bn = _tile(m, 1024), _tile(d, 512)
    nb = d // bn
    row = lambda i, j: (i, 0)
    w_specs = [pl.BlockSpec((None, MERGE_ROWS, bn), functools.partial(lambda i, j, r: (layer, r, j), r=r))
               for r in range(kk // MERGE_ROWS)]
    return pl.pallas_call(
        _merge_body,
        grid=(m // bm, nb),
        in_specs=[pl.BlockSpec((bm, o_gla.shape[1]), row),
                  pl.BlockSpec((bm, o_gmlp.shape[1]), row),
                  pl.BlockSpec((bm, o_na.shape[1]), row)] + w_specs +
                 [pl.BlockSpec((bm, bn), lambda i, j: (i, j)),
                  pl.BlockSpec((bm, bn), lambda i, j: (i, j + nb)),
                  pl.BlockSpec((bm, bn), lambda i, j: (i, j + 2 * nb))],
        out_specs=pl.BlockSpec((bm, bn), lambda i, j: (i, j)),
        out_shape=jax.ShapeDtypeStruct((m, d), BF16),
        compiler_params=_params("arbitrary", "arbitrary"),
        name="branch_merge",
    )(o_gla, o_gmlp, o_na, *([w_branch] * (kk // MERGE_ROWS)), gates, gates, gates)


GLA_ROWS = 256
GLA_HPS = 2


def _log_sigmoid(x):
    return jnp.minimum(x, 0.0) - jnp.log(1.0 + jnp.exp(-jnp.abs(x)))


def _dot_exact01(tri, x):
    hi = x.astype(BF16)
    rest = x - hi.astype(F32)
    mid = rest.astype(BF16)
    lo = (rest - mid.astype(F32)).astype(BF16)
    return (jnp.dot(tri, hi, preferred_element_type=F32) + jnp.dot(tri, mid, preferred_element_type=F32)
            + jnp.dot(tri, lo, preferred_element_type=F32))


def _gla_body(q_ref, k_ref, v_ref, g_ref, a_ref, w2f_ref, w2b_ref, bf_ref, bb_ref, gn_ref, o_ref,
              cf_s, cb_s, acc_s, sf_s, sb_s):
    seq = q_ref.shape[0]
    c = GLA_CHUNK
    kp, dv = GLA_DKP, GLA_DV
    n_chunks = seq // c
    rb = min(GLA_ROWS, seq)

    r = lax.broadcasted_iota(jnp.int32, (rb, rb), 0)
    cc = lax.broadcasted_iota(jnp.int32, (rb, rb), 1)
    same = lax.shift_right_logical(r, 6) == lax.shift_right_logical(cc, 6)
    tri_f = jnp.where(same & (cc <= r), 1.0, 0.0).astype(BF16)
    tri_b = jnp.where(same & (cc >= r), 1.0, 0.0).astype(BF16)

    def decay_step(i, carry):
        rows = pl.ds(pl.multiple_of(i * rb, rb), rb)
        a = a_ref[rows, :]
        la_f = _log_sigmoid(jnp.dot(a, w2f_ref[0], preferred_element_type=F32) + bf_ref[0]) * (1.0 / GLA_TAU)
        la_b = _log_sigmoid(jnp.dot(a, w2b_ref[0], preferred_element_type=F32) + bb_ref[0]) * (1.0 / GLA_TAU)
        cf_s[rows, :] = _dot_exact01(tri_f, la_f)
        cb_s[rows, :] = _dot_exact01(tri_b, la_b)
        acc_s[rows, :] = jnp.zeros((rb, acc_s.shape[1]), F32)
        return carry

    lax.fori_loop(0, seq // rb, decay_step, 0)
    sf_s[...] = jnp.zeros(sf_s.shape, F32)
    sb_s[...] = jnp.zeros(sb_s.shape, F32)

    ii = lax.broadcasted_iota(jnp.int32, (c, c), 0)
    jj = lax.broadcasted_iota(jnp.int32, (c, c), 1)
    mask_f = jj <= ii
    mask_b = jj > ii
    q_scale = GLA_DK ** -0.5

    def chunk(idx, hh, cum_s, state_s, mask, mid, last):
        rows = pl.ds(pl.multiple_of(idx * c, c), c)
        kcols = slice(hh * kp, (hh + 1) * kp)
        vcols = slice(hh * dv, (hh + 1) * dv)
        cum = cum_s[rows, kcols]
        b_mid = cum[mid:mid + 1, :]
        b_last = cum[last:last + 1, :]
        q = q_ref[rows, kcols].astype(F32) * q_scale
        k = k_ref[rows, kcols].astype(F32)
        v = v_ref[rows, vcols]
        q_intra = (q * jnp.exp(cum - b_mid)).astype(BF16)
        k_intra = (k * jnp.exp(b_mid - cum)).astype(BF16)
        scores = lax.dot_general(q_intra, k_intra, (((1,), (1,)), ((), ())), preferred_element_type=F32)
        scores = jnp.where(mask, scores, 0.0).astype(BF16)
        o = jnp.dot(scores, v, preferred_element_type=F32)
        state = state_s[hh]
        q_inter = (q * jnp.exp(cum)).astype(BF16)
        o += lax.dot_general(q_inter, state.astype(BF16), (((1,), (1,)), ((), ())), preferred_element_type=F32)
        k_state = (k * jnp.exp(b_last - cum)).astype(BF16)
        kv = lax.dot_general(v, k_state, (((0,), (0,)), ((), ())), preferred_element_type=F32)
        state_s[hh] = state * jnp.exp(b_last) + kv
        acc_s[rows, vcols] += o

    def scan_step(n, carry):
        for hh in range(sf_s.shape[0]):
            chunk(n, hh, cf_s, sf_s, mask_f, c // 2 - 1, c - 1)
            chunk(n_chunks - 1 - n, hh, cb_s, sb_s, mask_b, c // 2, 0)
        return carry

    lax.fori_loop(0, n_chunks, scan_step, 0, unroll=4)

    def out_step(i, carry):
        rows = pl.ds(pl.multiple_of(i * rb, rb), rb)
        for hh in range(sf_s.shape[0]):
            vcols = slice(hh * dv, (hh + 1) * dv)
            o = acc_s[rows, vcols]
            ms = jnp.mean(o * o, axis=-1, keepdims=True)
            g = g_ref[rows, vcols].astype(F32)
            o_ref[rows, vcols] = (o * lax.rsqrt(ms + EPS) * gn_ref[...] * (g * jax.nn.sigmoid(g))).astype(o_ref.dtype)
        return carry

    lax.fori_loop(0, seq // rb, out_step, 0)


def _gla(proj, w2f, w2b, bias_f, bias_b, out_norm, batch, seq):
    m = proj.shape[0]
    hps = GLA_HPS
    kb, vb = hps * GLA_DKP, hps * GLA_DV
    assert OFF_GQ % kb == 0 and OFF_GK % kb == 0 and OFF_GV % vb == 0 and OFF_GG % vb == 0
    return pl.pallas_call(
        _gla_body,
        grid=(batch, GLA_HEADS // hps),
        in_specs=[pl.BlockSpec((seq, kb), lambda b, h: (b, OFF_GQ // kb + h)),
                  pl.BlockSpec((seq, kb), lambda b, h: (b, OFF_GK // kb + h)),
                  pl.BlockSpec((seq, vb), lambda b, h: (b, OFF_GV // vb + h)),
                  pl.BlockSpec((seq, vb), lambda b, h: (b, OFF_GG // vb + h)),
                  pl.BlockSpec((seq, LANE), lambda b, h: (b, OFF_A // LANE)),
                  pl.BlockSpec((1, LANE, kb), lambda b, h: (h, 0, 0)),
                  pl.BlockSpec((1, LANE, kb), lambda b, h: (h, 0, 0)),
                  pl.BlockSpec((1, 1, kb), lambda b, h: (h, 0, 0)),
                  pl.BlockSpec((1, 1, kb), lambda b, h: (h, 0, 0)),
                  pl.BlockSpec((1, GLA_DV), lambda b, h: (0, 0))],
        out_specs=pl.BlockSpec((seq, vb), lambda b, h: (b, h)),
        out_shape=jax.ShapeDtypeStruct((m, GLA_WIDTH), BF16),
        scratch_shapes=[pltpu.VMEM((seq, kb), F32), pltpu.VMEM((seq, kb), F32),
                        pltpu.VMEM((seq, vb), F32),
                        pltpu.VMEM((hps, GLA_DV, GLA_DKP), F32), pltpu.VMEM((hps, GLA_DV, GLA_DKP), F32)],
        compiler_params=_params("arbitrary", "arbitrary"),
        name="gla_mixer",
    )(proj, proj, proj, proj, proj, w2f, w2b, bias_f, bias_b, out_norm.reshape(1, GLA_DV))


GMLP_ROWS = 512


def _gelu_tanh(x):
    return 0.5 * x * (1.0 + jnp.tanh(math.sqrt(2.0 / math.pi) * (x + 0.044715 * (x * x * x))))


def _gmlp_body(u_ref, v_ref, lg_ref, lb_ref, ws_ref, bs_ref, o_ref):
    t = GMLP_CHUNK
    for ci in range(u_ref.shape[0] // t):
        rows = slice(ci * t, (ci + 1) * t)
        vf = _gelu_tanh(v_ref[rows, :].astype(F32))
        mu = jnp.mean(vf, axis=-1, keepdims=True)
        var = jnp.mean(jnp.square(vf - mu), axis=-1, keepdims=True)
        vn = ((vf - mu) * lax.rsqrt(var + EPS) * lg_ref[...] + lb_ref[...]).astype(BF16)
        for g in range(GMLP_GROUPS):
            cols = slice(g * GMLP_GROUP_DIM, (g + 1) * GMLP_GROUP_DIM)
            s = jnp.dot(ws_ref[g], vn[:, cols], preferred_element_type=F32) + bs_ref[:, cols]
            u = _gelu_tanh(u_ref[rows, cols].astype(F32))
            o_ref[rows, cols] = (u * s).astype(o_ref.dtype)


def _gmlp(proj, ln_g, ln_b, w_s, bias):
    m = proj.shape[0]
    w = GMLP_WIDTH
    bm = _tile(m, GMLP_ROWS)
    return pl.pallas_call(
        _gmlp_body,
        grid=(m // bm,),
        in_specs=[pl.BlockSpec((bm, w), lambda i: (i, OFF_MU // w)),
                  pl.BlockSpec((bm, w), lambda i: (i, OFF_MV // w)),
                  pl.BlockSpec((1, w), lambda i: (0, 0)),
                  pl.BlockSpec((1, w), lambda i: (0, 0)),
                  pl.BlockSpec((GMLP_GROUPS, GMLP_CHUNK, GMLP_CHUNK), lambda i: (0, 0, 0)),
                  pl.BlockSpec((GMLP_CHUNK, w), lambda i: (0, 0))],
        out_specs=pl.BlockSpec((bm, w), lambda i: (i, 0)),
        out_shape=jax.ShapeDtypeStruct((m, w), BF16),
        compiler_params=_params("arbitrary"),
        name="gmlp_mixer",
    )(proj, proj, ln_g.reshape(1, w), ln_b.reshape(1, w), w_s, bias)


def _na_key_start(qb, rows):
    return int(np.clip(qb * NA_QROWS - NA_WIN_ROWS // 2, 0, rows - NA_KROWS))


def _na_table_index(qb, nqb):
    return 0 if qb == 0 else (2 if qb == nqb - 1 else 1)


def _na_body(q_ref, k_ref, v_ref, bias_ref, o_ref, *, rows):
    nqb = rows // NA_QROWS
    qn = NA_QROWS * GRID_W
    for qb in range(nqb):
        start = _na_key_start(qb, rows) * GRID_W
        keys = slice(start, start + NA_KROWS * GRID_W)
        qrows = slice(qb * qn, (qb + 1) * qn)
        s = lax.dot_general(q_ref[qrows, :], k_ref[keys, :], (((1,), (1,)), ((), ())),
                            preferred_element_type=F32)
        s = s * (NA_HEAD_DIM ** -0.5) + bias_ref[0, _na_table_index(qb, nqb)]
        p = jnp.exp(s - jnp.max(s, axis=-1, keepdims=True))
        l = jnp.sum(p, axis=-1, keepdims=True)
        o = jnp.dot(p.astype(BF16), v_ref[keys, :], preferred_element_type=F32)
        o_ref[qrows, :] = (o / l).astype(o_ref.dtype)


def _na_toeplitz(rpb):
    h, ny, nx = rpb.shape
    w = GRID_W
    left = w - NA_WIN_COLS
    v = jnp.pad(rpb, ((0, 0), (0, 0), (left, 2 * w - left - nx)))
    flat = jnp.broadcast_to(v[:, :, None, :], (h, ny, w, 2 * w)).reshape(h, ny, 2 * w * w)
    skew = flat[:, :, : w * (2 * w - 1)].reshape(h, ny, w, 2 * w - 1)
    return skew[..., w - 1:]


def _na_bias_table(rpb, rows):
    nqb = rows // NA_QROWS
    ny = 2 * NA_WIN_ROWS - 1
    toe = jnp.pad(_na_toeplitz(rpb), ((0, 0), (NA_KROWS, NA_KROWS), (0, 0), (0, 0)))
    qc = np.arange(GRID_W)[:, None]
    kc = np.arange(GRID_W)[None, :]
    c0 = np.clip(qc - NA_WIN_COLS // 2, 0, GRID_W - NA_WIN_COLS)
    col_ok = (kc >= c0) & (kc < c0 + NA_WIN_COLS)
    tables = []
    for qb in (0, 1, nqb - 1):
        start = _na_key_start(qb, rows)
        blocks, row_ok = [], []
        for a in range(NA_QROWS):
            qr = qb * NA_QROWS + a
            r0 = int(np.clip(qr - NA_WIN_ROWS // 2, 0, rows - NA_WIN_ROWS))
            kr = start + np.arange(NA_KROWS)
            row_ok.append((kr >= r0) & (kr < r0 + NA_WIN_ROWS))
            dy0 = start - qr + NA_WIN_ROWS - 1
            assert -NA_KROWS <= dy0 <= ny
            blocks.append(lax.slice_in_dim(toe, NA_KROWS + dy0, 2 * NA_KROWS + dy0, axis=1))
        blk = jnp.stack(blocks, axis=1)
        blk = blk.transpose(0, 1, 3, 2, 4)
        ok = np.stack(row_ok)[:, None, :, None] & col_ok[None, :, None, :]
        tab = jnp.where(ok[None], blk, -jnp.inf)
        tables.append(tab.reshape(rpb.shape[0], NA_QROWS * GRID_W, NA_KROWS * GRID_W))
    return jnp.stack(tables, axis=1).astype(F32)


def _na(proj, bias_table, batch, seq):
    m = proj.shape[0]
    rows = seq // GRID_W
    qn = NA_QROWS * GRID_W
    kn = NA_KROWS * GRID_W
    d = NA_HEAD_DIM
    return pl.pallas_call(
        functools.partial(_na_body, rows=rows),
        grid=(NA_HEADS, batch),
        in_specs=[pl.BlockSpec((seq, d), lambda h, b: (b, OFF_NQ // d + h)),
                  pl.BlockSpec((seq, d), lambda h, b: (b, OFF_NK // d + h)),
                  pl.BlockSpec((seq, d), lambda h, b: (b, OFF_NV // d + h)),
                  pl.BlockSpec((1, 3, qn, kn), lambda h, b: (h, 0, 0, 0))],
        out_specs=pl.BlockSpec((seq, d), lambda h, b: (b, h)),
        out_shape=jax.ShapeDtypeStruct((m, NA_WIDTH), BF16),
        compiler_params=_params("arbitrary", "arbitrary"),
        name="na_mixer",
    )(proj, proj, proj, bias_table)


def _pack_moves():
    src_off = np.concatenate([[0], np.cumsum((GLA_K_WIDTH, GLA_K_WIDTH, GLA_WIDTH, GLA_WIDTH, GLA_RANK, GLA_RANK,
                                              GMLP_WIDTH, GMLP_WIDTH, NA_WIDTH, NA_WIDTH, NA_WIDTH))]).tolist()
    s_gq, s_gk, s_gv, s_gg, s_af, s_ab, s_mu, s_mv, s_nq, s_nk, s_nv, _ = src_off
    moves = []
    for h in range(GLA_HEADS):
        moves.append((OFF_GQ + h * GLA_DKP, s_gq + h * GLA_DK, GLA_DK))
        moves.append((OFF_GK + h * GLA_DKP, s_gk + h * GLA_DK, GLA_DK))
    moves += [(OFF_MU, s_mu, GMLP_WIDTH), (OFF_GV, s_gv, GLA_WIDTH), (OFF_GG, s_gg, GLA_WIDTH),
              (OFF_NQ, s_nq, NA_WIDTH), (OFF_NK, s_nk, NA_WIDTH), (OFF_NV, s_nv, NA_WIDTH),
              (OFF_A, s_af, 2 * GLA_RANK), (OFF_MV, s_mv, GMLP_WIDTH)]
    return sorted(moves)


def _pack_in_body(w_ref, g_ref, o_ref):
    g = g_ref[...]
    cursor = 0
    for dst, src, width in _pack_moves():
        if dst > cursor:
            o_ref[cursor:dst, :] = jnp.zeros((dst - cursor, o_ref.shape[1]), o_ref.dtype)
        o_ref[dst:dst + width, :] = (w_ref[src:src + width, :] * g).astype(o_ref.dtype)
        cursor = dst + width
    if cursor < o_ref.shape[0]:
        o_ref[cursor:, :] = jnp.zeros((o_ref.shape[0] - cursor, o_ref.shape[1]), o_ref.dtype)


def _pack_w_in(w, g):
    depth, d, n = w.shape
    wt = jnp.swapaxes(w, 1, 2)
    bc = _tile(d, 256)
    return pl.pallas_call(
        _pack_in_body,
        grid=(depth, d // bc),
        in_specs=[pl.BlockSpec((None, n, bc), lambda l, i: (l, 0, i)),
                  pl.BlockSpec((None, 1, bc), lambda l, i: (l, 0, i))],
        out_specs=pl.BlockSpec((None, PROJ_WIDTH, bc), lambda l, i: (l, 0, i)),
        out_shape=jax.ShapeDtypeStruct((depth, PROJ_WIDTH, d), BF16),
        compiler_params=_params("arbitrary", "arbitrary"),
        name="pack_w_in",
    )(wt, g[:, None, :])


def _pack_decay(w2, bias, slot):
    groups = GLA_HEADS // GLA_HPS
    w = jnp.pad(w2.reshape(GLA_RANK, GLA_HEADS, GLA_DK), ((0, 0), (0, 0), (0, GLA_DKP - GLA_DK)))
    w = w.reshape(GLA_RANK, groups, GLA_HPS * GLA_DKP).transpose(1, 0, 2)
    w = jnp.pad(w, ((0, 0), (slot * GLA_RANK, LANE - (slot + 1) * GLA_RANK), (0, 0)))
    b = jnp.pad(bias.reshape(GLA_HEADS, GLA_DK), ((0, 0), (0, GLA_DKP - GLA_DK)))
    return w.astype(BF16), b.reshape(groups, 1, GLA_HPS * GLA_DKP).astype(F32)


def _scaled_bf16(w, g):
    return (w * g[:, :, None]).astype(BF16)


def _cast_pad_body(w_ref, o_ref, *, valid_rows, scale):
    br = o_ref.shape[0]
    row = pl.program_id(0) * br + lax.broadcasted_iota(jnp.int32, (br, 1), 0)
    o_ref[...] = jnp.where(row < valid_rows, w_ref[...] * scale, 0.0).astype(o_ref.dtype)


def _cast_pad_rows(w, layer, rows, scale):
    _, k, n = w.shape
    br = _tile(rows, 512)
    assert rows - k < br
    return pl.pallas_call(
        functools.partial(_cast_pad_body, valid_rows=k, scale=scale),
        grid=(rows // br,),
        in_specs=[pl.BlockSpec((None, br, n), lambda i: (layer, i, 0))],
        out_specs=pl.BlockSpec((None, br, n), lambda i: (0, i, 0)),
        out_shape=jax.ShapeDtypeStruct((1, rows, n), BF16),
        compiler_params=_params("arbitrary"),
        name="cast_pad_rows",
    )(w)


def _mixer(x, xb, ss, batch, seq, layer, w_in_p, w_gate_b, w_a2_f, b_f, w_a2_b, b_b, out_norm,
           ln_g, ln_b, w_s, b_s, rpb, w_branch_b, w_out_b):
    proj = _normed_linear(xb, ss, w_in_p, layer, None, "mixer_in_proj", w_transposed=True)
    gates = _normed_linear(xb, ss, w_gate_b, layer, "sigmoid", "mixer_gates")

    w2f, bias_f = _pack_decay(w_a2_f, b_f, 0)
    w2b, bias_b = _pack_decay(w_a2_b, b_b, 1)
    o_gla = _gla(proj, w2f, w2b, bias_f, bias_b, out_norm, batch, seq)

    gmlp_bias = jnp.repeat(b_s.T, GMLP_GROUP_DIM, axis=1)
    o_gmlp = _gmlp(proj, ln_g, ln_b, w_s.astype(BF16), gmlp_bias)

    o_na = _na(proj, _na_bias_table(rpb, seq // GRID_W), batch, seq)

    y = _merge(o_gla, o_gmlp, o_na, w_branch_b, layer, gates)
    return _residual_linear(y, w_out_b, layer, x, 512, 4096, True, "mixer_out")


def kernel(x, ffn1_norm, ffn1_w_in, ffn1_w_out, mix_norm, w_in, w_gate, gla_w_a2_fwd, gla_b_fwd, gla_w_a2_bwd, gla_b_bwd, gla_out_norm, gmlp_ln_g, gmlp_ln_b, gmlp_w_s, gmlp_b_s, na_rpb, w_branch, w_out, ffn2_norm, ffn2_w_in, ffn2_w_out, final_norm):
    batch, seq, d = x.shape
    depth = ffn1_norm.shape[0]
    assert seq % (NA_QROWS * GRID_W) == 0 and seq // GRID_W >= NA_KROWS and seq % GMLP_CHUNK == 0
    f_pad = _round_up(ffn1_w_out.shape[1], 1024)

    ffns = [w for l in range(depth) for w in ((ffn1_w_in, ffn1_w_out, ffn1_norm, l), (ffn2_w_in, ffn2_w_out, ffn2_norm, l))]

    def in_rider(i):
        w_i, _, norm, l = ffns[i]
        return _Rider(w_i, l, norm[:, :, None], 1.0, w_i.shape[1])

    def out_rider(i):
        return _Rider(ffns[i][1], ffns[i][3], None, 0.5, f_pad)

    def convert_in(i):
        w_i, _, norm, l = ffns[i]
        return _scaled_bf16(w_i[l:l + 1], norm[l:l + 1])

    def convert_out(i):
        return _cast_pad_rows(ffns[i][1], ffns[i][3], f_pad, 0.5)

    w_in_p = _pack_w_in(w_in, mix_norm)
    w_gate_b = _scaled_bf16(w_gate, mix_norm)
    w_branch_b, w_out_b = w_branch.astype(BF16), w_out.astype(BF16)

    xs = x.reshape(batch * seq, d)
    xb, ss = _prenorm(xs)
    w_i, w_o = convert_in(0), convert_out(0)
    for i in range(len(ffns)):
        has_next = i + 1 < len(ffns)
        if i % 2 == 1:
            l = i // 2
            xs, xb, ss = _mixer(xs, xb, ss, batch, seq, l, w_in_p, w_gate_b, gla_w_a2_fwd[l], gla_b_fwd[l],
                                gla_w_a2_bwd[l], gla_b_bwd[l], gla_out_norm[l], gmlp_ln_g[l], gmlp_ln_b[l],
                                gmlp_w_s[l], gmlp_b_s[l], na_rpb[l], w_branch_b, w_out_b)
        riders = (in_rider(i + 1), out_rider(i + 1)) if has_next else ()
        act, converted = _ffn_in(xb, ss, w_i, 0, f_pad, riders)
        outs = _residual_linear(act, w_o, 0, xs, 1024, 2816, has_next, "ffn_out")
        if has_next:
            xs, xb, ss = outs
            w_i = convert_in(i + 1) if converted[0] is None else converted[0][None]
            w_o = convert_out(i + 1) if converted[1] is None else converted[1][None]
        else:
            xs = outs[0]
    return _rmsnorm(xs, final_norm, F32).reshape(batch, seq, d)
```

```python
import functools
import math
import typing

import numpy as np
import jax
import jax.numpy as jnp
from jax import lax
from jax.experimental import pallas as pl
from jax.experimental.pallas import tpu as pltpu

F32 = jnp.float32
BF16 = jnp.bfloat16
EPS = 1e-6

GRID_W = 64
GLA_HEADS, GLA_DK, GLA_DV, GLA_RANK, GLA_TAU, GLA_CHUNK = 4, 192, 384, 16, 16.0, 64
GLA_DKP = 256
GMLP_GROUPS, GMLP_GROUP_DIM, GMLP_CHUNK = 8, 128, 128
GMLP_WIDTH = GMLP_GROUPS * GMLP_GROUP_DIM
NA_HEADS, NA_HEAD_DIM, NA_WIN_ROWS, NA_WIN_COLS = 12, 128, 8, 16
NA_WIDTH = NA_HEADS * NA_HEAD_DIM
GLA_K_WIDTH = GLA_HEADS * GLA_DK
GLA_WIDTH = GLA_HEADS * GLA_DV
NA_QROWS = 4
NA_KROWS = NA_QROWS + NA_WIN_ROWS

OFF_GQ, OFF_GK, OFF_MU, OFF_GV, OFF_GG = 0, 1024, 2048, 3072, 4608
OFF_NQ, OFF_NK, OFF_NV, OFF_A, OFF_MV = 6144, 7680, 9216, 10752, 11264
PROJ_WIDTH = 12288
LANE = 128

VMEM_LIMIT = 56 * 1024 * 1024


def _round_up(x, m):
    return (x + m - 1) // m * m


def _tile(dim, pref):
    if dim <= pref:
        return dim
    t = pref - pref % LANE
    while dim % t:
        t -= LANE
    return t


def _params(*sem):
    return pltpu.CompilerParams(dimension_semantics=sem, vmem_limit_bytes=VMEM_LIMIT)


def _lane_group_sum(x):
    out = x[:, :LANE]
    for c in range(1, x.shape[1] // LANE):
        out = out + x[:, c * LANE:(c + 1) * LANE]
    return out


def _rstd(ss_ref, d):
    return lax.rsqrt(jnp.sum(ss_ref[...], axis=-1, keepdims=True) * (1.0 / d) + EPS)


def _prenorm_body(x_ref, xb_ref, ss_ref):
    x = x_ref[...]
    xb_ref[...] = x.astype(xb_ref.dtype)
    ss_ref[...] = _lane_group_sum(x * x)


def _prenorm(x):
    m, d = x.shape
    bm = _tile(m, 256)
    return pl.pallas_call(
        _prenorm_body,
        grid=(m // bm,),
        in_specs=[pl.BlockSpec((bm, d), lambda i: (i, 0))],
        out_specs=[pl.BlockSpec((bm, d), lambda i: (i, 0)),
                   pl.BlockSpec((bm, LANE), lambda i: (i, 0))],
        out_shape=[jax.ShapeDtypeStruct((m, d), BF16), jax.ShapeDtypeStruct((m, LANE), F32)],
        compiler_params=_params("arbitrary"),
        name="prenorm",
    )(x)


def _rmsnorm_body(x_ref, g_ref, o_ref):
    x = x_ref[...]
    ms = jnp.mean(x * x, axis=-1, keepdims=True)
    o_ref[...] = (x * lax.rsqrt(ms + EPS) * g_ref[...]).astype(o_ref.dtype)


def _rmsnorm(x, g, out_dtype):
    m, d = x.shape
    bm = _tile(m, 256)
    return pl.pallas_call(
        _rmsnorm_body,
        grid=(m // bm,),
        in_specs=[pl.BlockSpec((bm, d), lambda i: (i, 0)),
                  pl.BlockSpec((1, d), lambda i: (0, 0))],
        out_specs=pl.BlockSpec((bm, d), lambda i: (i, 0)),
        out_shape=jax.ShapeDtypeStruct((m, d), out_dtype),
        compiler_params=_params("arbitrary"),
        name="rmsnorm",
    )(x, g.reshape(1, d))


def _normed_linear_body(x_ref, ss_ref, w_ref, o_ref, *, act, w_transposed):
    contract = (((1,), (1 if w_transposed else 0,)), ((), ()))
    acc = _rstd(ss_ref, x_ref.shape[1]) * lax.dot_general(x_ref[...], w_ref[...], contract,
                                                          preferred_element_type=F32)
    if act == "sigmoid":
        acc = _sigmoid(acc)
    o_ref[...] = acc.astype(o_ref.dtype)


def _normed_linear(xb, ss, w, layer, act, name, w_transposed=False):
    m, k = xb.shape
    n = w.shape[1] if w_transposed else w.shape[2]
    bm, bn = _tile(m, 1024), _tile(n, 1024)
    w_spec = (pl.BlockSpec((None, bn, k), lambda i, j: (layer, j, 0)) if w_transposed
              else pl.BlockSpec((None, k, bn), lambda i, j: (layer, 0, j)))
    return pl.pallas_call(
        functools.partial(_normed_linear_body, act=act, w_transposed=w_transposed),
        grid=(m // bm, n // bn),
        in_specs=[pl.BlockSpec((bm, k), lambda i, j: (i, 0)),
                  pl.BlockSpec((bm, ss.shape[1]), lambda i, j: (i, 0)),
                  w_spec],
        out_specs=pl.BlockSpec((bm, bn), lambda i, j: (i, j)),
        out_shape=jax.ShapeDtypeStruct((m, n), BF16),
        compiler_params=_params("arbitrary", "arbitrary"),
        name=name,
    )(xb, ss, w)


RIDER_BLOCK_BYTES = 3 * 1024 * 1024


class _Rider(typing.NamedTuple):
    src: jax.Array
    layer: int
    gain: typing.Optional[jax.Array]
    scale: float
    out_rows: int


def _rider_rows(rider, steps):
    _, r, c = rider.src.shape
    for rr in range(16, min(RIDER_BLOCK_BYTES // (4 * c), r) + 1, 16):
        if r % rr == 0 and rider.out_rows % rr == 0 and rider.out_rows // rr <= steps:
            return rr
    return None


def _rider_specs(rider, rr, step_of):
    _, r, c = rider.src.shape
    n_valid, n_out = r // rr, rider.out_rows // rr
    dst_idx = lambda *g: jnp.minimum(step_of(*g), n_out - 1)
    src_map = lambda *g: (rider.layer, jnp.minimum(dst_idx(*g), n_valid - 1), 0)
    in_specs, in_args = [pl.BlockSpec((None, rr, c), src_map)], [rider.src]
    if rider.gain is not None:
        in_specs.append(pl.BlockSpec((None, rr, 1), src_map))
        in_args.append(rider.gain)
    out_spec = pl.BlockSpec((rr, c), lambda *g: (dst_idx(*g), 0))
    return in_specs, in_args, out_spec, jax.ShapeDtypeStruct((rider.out_rows, c), BF16)


def _rider_step(refs, step, n_valid, n_out, scale):
    src_ref, dst_ref = refs[0], refs[-1]
    v = src_ref[...]
    if len(refs) == 3:
        v = v * refs[1][...]
    if scale != 1.0:
        v = v * scale
    if n_out > n_valid:
        v = jnp.where(jnp.minimum(step, n_out - 1) < n_valid, v, 0.0)
    dst_ref[...] = v.astype(dst_ref.dtype)


def _sigmoid(x):
    return 0.5 * jnp.tanh(0.5 * x) + 0.5


def _ffn_in_body(x_ref, ss_ref, wa_ref, wb0_ref, wb1_ref, *refs, width, tail, riders):
    n_rider_in = sum(meta[0] for meta in riders)
    o_ref = refs[n_rider_in]
    bn = o_ref.shape[1]
    half = bn // 2

    def ride():
        step = pl.program_id(0) * pl.num_programs(1) + pl.program_id(1)
        first = 0
        for r, meta in enumerate(riders):
            _rider_step(refs[first:first + meta[0]] + (refs[n_rider_in + 1 + r],), step, *meta[1:])
            first += meta[0]

    def compute(valid):
        ride()
        x = x_ref[...]
        rstd = _rstd(ss_ref, x_ref.shape[1])
        a = rstd * jnp.dot(x, wa_ref[:, :valid], preferred_element_type=F32)
        for c, wb_ref in enumerate((wb0_ref, wb1_ref)[:valid // half]):
            cols = slice(c * half, (c + 1) * half)
            b = rstd * jnp.dot(x, wb_ref[...], preferred_element_type=F32)
            ac = a[:, cols]
            o_ref[:, cols] = (ac * _sigmoid(ac) * b).astype(o_ref.dtype)
        if valid < bn:
            o_ref[:, valid:] = jnp.zeros((o_ref.shape[0], bn - valid), o_ref.dtype)

    if tail == bn:
        compute(bn)
    else:
        is_tail = (pl.program_id(1) + 1) * bn > width
        pl.when(jnp.logical_not(is_tail))(lambda: compute(bn))
        pl.when(is_tail)(lambda: compute(tail))


def _ffn_in(xb, ss, w, layer, f_pad, riders=()):
    m, k = xb.shape
    f = w.shape[2] // 2
    bm, bn = _tile(m, 1024), _tile(f_pad, 512)
    half = bn // 2
    nb = f // half
    last = 2 * nb - 1
    tail = f - (f_pad // bn - 1) * bn
    assert f % half == 0 and tail in (half, bn)
    grid = (m // bm, f_pad // bn)
    in_specs = [pl.BlockSpec((bm, k), lambda i, j: (i, 0)),
                pl.BlockSpec((bm, ss.shape[1]), lambda i, j: (i, 0)),
                pl.BlockSpec((None, k, bn), lambda i, j: (layer, 0, j)),
                pl.BlockSpec((None, k, half), lambda i, j: (layer, 0, jnp.minimum(nb + 2 * j, last))),
                pl.BlockSpec((None, k, half), lambda i, j: (layer, 0, jnp.minimum(nb + 2 * j + 1, last)))]
    args = [xb, ss, w, w, w]
    out_specs = [pl.BlockSpec((bm, bn), lambda i, j: (i, j))]
    out_shape = [jax.ShapeDtypeStruct((m, f_pad), BF16)]
    metas, hosted = [], []
    for rider in riders:
        rr = _rider_rows(rider, grid[0] * grid[1])
        hosted.append(rr is not None)
        if rr is None:
            continue
        r_in, r_args, r_out, r_shape = _rider_specs(rider, rr, lambda i, j: i * grid[1] + j)
        in_specs += r_in
        args += r_args
        out_specs.append(r_out)
        out_shape.append(r_shape)
        metas.append((len(r_in), rider.src.shape[1] // rr, rider.out_rows // rr, rider.scale))
    out = pl.pallas_call(
        functools.partial(_ffn_in_body, width=f, tail=tail, riders=tuple(metas)),
        grid=grid,
        in_specs=in_specs,
        out_specs=out_specs,
        out_shape=out_shape,
        compiler_params=_params("arbitrary", "arbitrary"),
        name="ffn_in",
    )(*args)
    converted = iter(out[1:])
    return out[0], [next(converted) if h else None for h in hosted]


def _residual_body(a_ref, w_ref, x_ref, o_ref, *norm_refs):
    nk = pl.num_programs(2)

    @pl.when(pl.program_id(2) == 0)
    def _():
        o_ref[...] = x_ref[...]

    o_ref[...] += jnp.dot(a_ref[...], w_ref[...], preferred_element_type=F32)

    if norm_refs:
        xb_ref, ss_ref = norm_refs

        @pl.when(pl.program_id(2) == nk - 1)
        def _():
            xn = o_ref[...]
            xb_ref[...] = xn.astype(xb_ref.dtype)
            part = _lane_group_sum(xn * xn)

            @pl.when(pl.program_id(1) == 0)
            def _():
                ss_ref[...] = part

            @pl.when(pl.program_id(1) > 0)
            def _():
                ss_ref[...] += part


def _residual_linear(a, w, layer, x, bn_pref, tk_pref, emit_norm, name):
    m, k = a.shape
    n = w.shape[2]
    bm, bn, tk = _tile(m, 1024), _tile(n, bn_pref), _tile(k, tk_pref)
    out_specs = [pl.BlockSpec((bm, bn), lambda i, j, kk: (i, j))]
    out_shape = [jax.ShapeDtypeStruct((m, n), F32)]
    if emit_norm:
        out_specs += [pl.BlockSpec((bm, bn), lambda i, j, kk: (i, j)),
                      pl.BlockSpec((bm, LANE), lambda i, j, kk: (i, 0))]
        out_shape += [jax.ShapeDtypeStruct((m, n), BF16), jax.ShapeDtypeStruct((m, LANE), F32)]
    return pl.pallas_call(
        _residual_body,
        grid=(m // bm, n // bn, k // tk),
        in_specs=[pl.BlockSpec((bm, tk), lambda i, j, kk: (i, kk)),
                  pl.BlockSpec((None, tk, bn), lambda i, j, kk: (layer, kk, j)),
                  pl.BlockSpec((bm, bn), lambda i, j, kk: (i, j))],
        out_specs=out_specs,
        out_shape=out_shape,
        compiler_params=_params("arbitrary", "arbitrary", "arbitrary"),
        name=name,
    )(a, w, x)


MERGE_ROWS = 512


def _merge_body(og_ref, om_ref, on_ref, *refs):
    nblk = (og_ref.shape[1] + om_ref.shape[1] + on_ref.shape[1]) // MERGE_ROWS
    w_refs, (g1_ref, g2_ref, g3_ref, y_ref) = refs[:nblk], refs[nblk:]
    y = None
    first = 0
    for o_ref, g_ref in ((og_ref, g1_ref), (om_ref, g2_ref), (on_ref, g3_ref)):
        cnt = o_ref.shape[1] // MERGE_ROWS
        w = jnp.concatenate([r[...] for r in w_refs[first:first + cnt]], axis=0)
        first += cnt
        term = g_ref[...].astype(F32) * jnp.dot(o_ref[...], w, preferred_element_type=F32)
        y = term if y is None else y + term
    y_ref[...] = y.astype(y_ref.dtype)


def _merge(o_gla, o_gmlp, o_na, w_branch, layer, gates):
    m = o_gla.shape[0]
    kk, d = w_branch.shape[1:]
    bm, bn = _tile(m, 1024), _tile(d, 512)
    nb = d // bn
    row = lambda i, j: (i, 0)
    w_specs = [pl.BlockSpec((None, MERGE_ROWS, bn), functools.partial(lambda i, j, r: (layer, r, j), r=r))
               for r in range(kk // MERGE_ROWS)]
    return pl.pallas_call(
        _merge_body,
        grid=(m // bm, nb),
        in_specs=[pl.BlockSpec((bm, o_gla.shape[1]), row),
                  pl.BlockSpec((bm, o_gmlp.shape[1]), row),
                  pl.BlockSpec((bm, o_na.shape[1]), row)] + w_specs +
                 [pl.BlockSpec((bm, bn), lambda i, j: (i, j)),
                  pl.BlockSpec((bm, bn), lambda i, j: (i, j + nb)),
                  pl.BlockSpec((bm, bn), lambda i, j: (i, j + 2 * nb))],
        out_specs=pl.BlockSpec((bm, bn), lambda i, j: (i, j)),
        out_shape=jax.ShapeDtypeStruct((m, d), BF16),
        compiler_params=_params("arbitrary", "arbitrary"),
        name="branch_merge",
    )(o_gla, o_gmlp, o_na, *([w_branch] * (kk // MERGE_ROWS)), gates, gates, gates)


GLA_ROWS = 256
GLA_HPS = 2


def _log_sigmoid(x):
    return jnp.minimum(x, 0.0) - jnp.log(1.0 + jnp.exp(-jnp.abs(x)))


def _dot_exact01(tri, x):
    hi = x.astype(BF16)
    rest = x - hi.astype(F32)
    mid = rest.astype(BF16)
    lo = (rest - mid.astype(F32)).astype(BF16)
    return (jnp.dot(tri, hi, preferred_element_type=F32) + jnp.dot(tri, mid, preferred_element_type=F32)
            + jnp.dot(tri, lo, preferred_element_type=F32))


def _gla_body(q_ref, k_ref, v_ref, g_ref, a_ref, w2f_ref, w2b_ref, bf_ref, bb_ref, gn_ref, o_ref,
              cf_s, cb_s, acc_s, sf_s, sb_s):
    seq = q_ref.shape[0]
    c = GLA_CHUNK
    kp, dv = GLA_DKP, GLA_DV
    n_chunks = seq // c
    rb = min(GLA_ROWS, seq)

    r = lax.broadcasted_iota(jnp.int32, (rb, rb), 0)
    cc = lax.broadcasted_iota(jnp.int32, (rb, rb), 1)
    same = lax.shift_right_logical(r, 6) == lax.shift_right_logical(cc, 6)
    tri_f = jnp.where(same & (cc <= r), 1.0, 0.0).astype(BF16)
    tri_b = jnp.where(same & (cc >= r), 1.0, 0.0).astype(BF16)

    def decay_step(i, carry):
        rows = pl.ds(pl.multiple_of(i * rb, rb), rb)
        a = a_ref[rows, :]
        la_f = _log_sigmoid(jnp.dot(a, w2f_ref[0], preferred_element_type=F32) + bf_ref[0]) * (1.0 / GLA_TAU)
        la_b = _log_sigmoid(jnp.dot(a, w2b_ref[0], preferred_element_type=F32) + bb_ref[0]) * (1.0 / GLA_TAU)
        cf_s[rows, :] = _dot_exact01(tri_f, la_f)
        cb_s[rows, :] = _dot_exact01(tri_b, la_b)
        acc_s[rows, :] = jnp.zeros((rb, acc_s.shape[1]), F32)
        return carry

    lax.fori_loop(0, seq // rb, decay_step, 0)
    sf_s[...] = jnp.zeros(sf_s.shape, F32)
    sb_s[...] = jnp.zeros(sb_s.shape, F32)

    ii = lax.broadcasted_iota(jnp.int32, (c, c), 0)
    jj = lax.broadcasted_iota(jnp.int32, (c, c), 1)
    mask_f = jj <= ii
    mask_b = jj > ii
    q_scale = GLA_DK ** -0.5

    def chunk(idx, hh, cum_s, state_s, mask, mid, last):
        rows = pl.ds(pl.multiple_of(idx * c, c), c)
        kcols = slice(hh * kp, (hh + 1) * kp)
        vcols = slice(hh * dv, (hh + 1) * dv)
        cum = cum_s[rows, kcols]
        b_mid = cum[mid:mid + 1, :]
        b_last = cum[last:last + 1, :]
        q = q_ref[rows, kcols].astype(F32) * q_scale
        k = k_ref[rows, kcols].astype(F32)
        v = v_ref[rows, vcols]
        q_intra = (q * jnp.exp(cum - b_mid)).astype(BF16)
        k_intra = (k * jnp.exp(b_mid - cum)).astype(BF16)
        scores = lax.dot_general(q_intra, k_intra, (((1,), (1,)), ((), ())), preferred_element_type=F32)
        scores = jnp.where(mask, scores, 0.0).astype(BF16)
        o = jnp.dot(scores, v, preferred_element_type=F32)
        state = state_s[hh]
        q_inter = (q * jnp.exp(cum)).astype(BF16)
        o += lax.dot_general(q_inter, state.astype(BF16), (((1,), (1,)), ((), ())), preferred_element_type=F32)
        k_state = (k * jnp.exp(b_last - cum)).astype(BF16)
        kv = lax.dot_general(v, k_state, (((0,), (0,)), ((), ())), preferred_element_type=F32)
        state_s[hh] = state * jnp.exp(b_last) + kv
        acc_s[rows, vcols] += o

    def scan_step(n, carry):
        for hh in range(sf_s.shape[0]):
            chunk(n, hh, cf_s, sf_s, mask_f, c // 2 - 1, c - 1)
            chunk(n_chunks - 1 - n, hh, cb_s, sb_s, mask_b, c // 2, 0)
        return carry

    lax.fori_loop(0, n_chunks, scan_step, 0, unroll=4)

    def out_step(i, carry):
        rows = pl.ds(pl.multiple_of(i * rb, rb), rb)
        for hh in range(sf_s.shape[0]):
            vcols = slice(hh * dv, (hh + 1) * dv)
            o = acc_s[rows, vcols]
            ms = jnp.mean(o * o, axis=-1, keepdims=True)
            g = g_ref[rows, vcols].astype(F32)
            o_ref[rows, vcols] = (o * lax.rsqrt(ms + EPS) * gn_ref[...] * (g * _sigmoid(g))).astype(o_ref.dtype)
        return carry

    lax.fori_loop(0, seq // rb, out_step, 0)


def _gla(proj, w2f, w2b, bias_f, bias_b, out_norm, batch, seq):
    m = proj.shape[0]
    hps = GLA_HPS
    kb, vb = hps * GLA_DKP, hps * GLA_DV
    assert OFF_GQ % kb == 0 and OFF_GK % kb == 0 and OFF_GV % vb == 0 and OFF_GG % vb == 0
    return pl.pallas_call(
        _gla_body,
        grid=(batch, GLA_HEADS // hps),
        in_specs=[pl.BlockSpec((seq, kb), lambda b, h: (b, OFF_GQ // kb + h)),
                  pl.BlockSpec((seq, kb), lambda b, h: (b, OFF_GK // kb + h)),
                  pl.BlockSpec((seq, vb), lambda b, h: (b, OFF_GV // vb + h)),
                  pl.BlockSpec((seq, vb), lambda b, h: (b, OFF_GG // vb + h)),
                  pl.BlockSpec((seq, LANE), lambda b, h: (b, OFF_A // LANE)),
                  pl.BlockSpec((1, LANE, kb), lambda b, h: (h, 0, 0)),
                  pl.BlockSpec((1, LANE, kb), lambda b, h: (h, 0, 0)),
                  pl.BlockSpec((1, 1, kb), lambda b, h: (h, 0, 0)),
                  pl.BlockSpec((1, 1, kb), lambda b, h: (h, 0, 0)),
                  pl.BlockSpec((1, GLA_DV), lambda b, h: (0, 0))],
        out_specs=pl.BlockSpec((seq, vb), lambda b, h: (b, h)),
        out_shape=jax.ShapeDtypeStruct((m, GLA_WIDTH), BF16),
        scratch_shapes=[pltpu.VMEM((seq, kb), F32), pltpu.VMEM((seq, kb), F32),
                        pltpu.VMEM((seq, vb), F32),
                        pltpu.VMEM((hps, GLA_DV, GLA_DKP), F32), pltpu.VMEM((hps, GLA_DV, GLA_DKP), F32)],
        compiler_params=_params("arbitrary", "arbitrary"),
        name="gla_mixer",
    )(proj, proj, proj, proj, proj, w2f, w2b, bias_f, bias_b, out_norm.reshape(1, GLA_DV))


GMLP_ROWS = 512


def _gelu_tanh(x):
    return 0.5 * x * (1.0 + jnp.tanh(math.sqrt(2.0 / math.pi) * (x + 0.044715 * (x * x * x))))


def _gmlp_body(u_ref, v_ref, lg_ref, lb_ref, ws_ref, bs_ref, o_ref):
    t = GMLP_CHUNK
    for ci in range(u_ref.shape[0] // t):
        rows = slice(ci * t, (ci + 1) * t)
        vf = _gelu_tanh(v_ref[rows, :].astype(F32))
        mu = jnp.mean(vf, axis=-1, keepdims=True)
        var = jnp.mean(jnp.square(vf - mu), axis=-1, keepdims=True)
        vn = ((vf - mu) * lax.rsqrt(var + EPS) * lg_ref[...] + lb_ref[...]).astype(BF16)
        for g in range(GMLP_GROUPS):
            cols = slice(g * GMLP_GROUP_DIM, (g + 1) * GMLP_GROUP_DIM)
            s = jnp.dot(ws_ref[g], vn[:, cols], preferred_element_type=F32) + bs_ref[:, cols]
            u = _gelu_tanh(u_ref[rows, cols].astype(F32))
            o_ref[rows, cols] = (u * s).astype(o_ref.dtype)


def _gmlp(proj, ln_g, ln_b, w_s, bias):
    m = proj.shape[0]
    w = GMLP_WIDTH
    bm = _tile(m, GMLP_ROWS)
    return pl.pallas_call(
        _gmlp_body,
        grid=(m // bm,),
        in_specs=[pl.BlockSpec((bm, w), lambda i: (i, OFF_MU // w)),
                  pl.BlockSpec((bm, w), lambda i: (i, OFF_MV // w)),
                  pl.BlockSpec((1, w), lambda i: (0, 0)),
                  pl.BlockSpec((1, w), lambda i: (0, 0)),
                  pl.BlockSpec((GMLP_GROUPS, GMLP_CHUNK, GMLP_CHUNK), lambda i: (0, 0, 0)),
                  pl.BlockSpec((GMLP_CHUNK, w), lambda i: (0, 0))],
        out_specs=pl.BlockSpec((bm, w), lambda i: (i, 0)),
        out_shape=jax.ShapeDtypeStruct((m, w), BF16),
        compiler_params=_params("arbitrary"),
        name="gmlp_mixer",
    )(proj, proj, ln_g.reshape(1, w), ln_b.reshape(1, w), w_s, bias)


def _na_key_start(qb, rows):
    return int(np.clip(qb * NA_QROWS - NA_WIN_ROWS // 2, 0, rows - NA_KROWS))


def _na_table_index(qb, nqb):
    return 0 if qb == 0 else (2 if qb == nqb - 1 else 1)


def _na_body(q_ref, k_ref, v_ref, bias_ref, o_ref, *, rows):
    nqb = rows // NA_QROWS
    qn = NA_QROWS * GRID_W
    for qb in range(nqb):
        start = _na_key_start(qb, rows) * GRID_W
        keys = slice(start, start + NA_KROWS * GRID_W)
        qrows = slice(qb * qn, (qb + 1) * qn)
        s = lax.dot_general(q_ref[qrows, :], k_ref[keys, :], (((1,), (1,)), ((), ())),
                            preferred_element_type=F32)
        s = s * (NA_HEAD_DIM ** -0.5) + bias_ref[0, _na_table_index(qb, nqb)]
        p = jnp.exp(s - jnp.max(s, axis=-1, keepdims=True))
        l = jnp.sum(p, axis=-1, keepdims=True)
        o = jnp.dot(p.astype(BF16), v_ref[keys, :], preferred_element_type=F32)
        o_ref[qrows, :] = (o / l).astype(o_ref.dtype)


def _na_toeplitz(rpb):
    h, ny, nx = rpb.shape
    w = GRID_W
    left = w - NA_WIN_COLS
    v = jnp.pad(rpb, ((0, 0), (0, 0), (left, 2 * w - left - nx)))
    flat = jnp.broadcast_to(v[:, :, None, :], (h, ny, w, 2 * w)).reshape(h, ny, 2 * w * w)
    skew = flat[:, :, : w * (2 * w - 1)].reshape(h, ny, w, 2 * w - 1)
    return skew[..., w - 1:]


def _na_bias_table(rpb, rows):
    nqb = rows // NA_QROWS
    ny = 2 * NA_WIN_ROWS - 1
    toe = jnp.pad(_na_toeplitz(rpb), ((0, 0), (NA_KROWS, NA_KROWS), (0, 0), (0, 0)))
    qc = np.arange(GRID_W)[:, None]
    kc = np.arange(GRID_W)[None, :]
    c0 = np.clip(qc - NA_WIN_COLS // 2, 0, GRID_W - NA_WIN_COLS)
    col_ok = (kc >= c0) & (kc < c0 + NA_WIN_COLS)
    tables = []
    for qb in (0, 1, nqb - 1):
        start = _na_key_start(qb, rows)
        blocks, row_ok = [], []
        for a in range(NA_QROWS):
            qr = qb * NA_QROWS + a
            r0 = int(np.clip(qr - NA_WIN_ROWS // 2, 0, rows - NA_WIN_ROWS))
            kr = start + np.arange(NA_KROWS)
            row_ok.append((kr >= r0) & (kr < r0 + NA_WIN_ROWS))
            dy0 = start - qr + NA_WIN_ROWS - 1
            assert -NA_KROWS <= dy0 <= ny
            blocks.append(lax.slice_in_dim(toe, NA_KROWS + dy0, 2 * NA_KROWS + dy0, axis=1))
        blk = jnp.stack(blocks, axis=1)
        blk = blk.transpose(0, 1, 3, 2, 4)
        ok = np.stack(row_ok)[:, None, :, None] & col_ok[None, :, None, :]
        tab = jnp.where(ok[None], blk, -jnp.inf)
        tables.append(tab.reshape(rpb.shape[0], NA_QROWS * GRID_W, NA_KROWS * GRID_W))
    return jnp.stack(tables, axis=1).astype(F32)


def _na(proj, bias_table, batch, seq):
    m = proj.shape[0]
    rows = seq // GRID_W
    qn = NA_QROWS * GRID_W
    kn = NA_KROWS * GRID_W
    d = NA_HEAD_DIM
    return pl.pallas_call(
        functools.partial(_na_body, rows=rows),
        grid=(NA_HEADS, batch),
        in_specs=[pl.BlockSpec((seq, d), lambda h, b: (b, OFF_NQ // d + h)),
                  pl.BlockSpec((seq, d), lambda h, b: (b, OFF_NK // d + h)),
                  pl.BlockSpec((seq, d), lambda h, b: (b, OFF_NV // d + h)),
                  pl.BlockSpec((1, 3, qn, kn), lambda h, b: (h, 0, 0, 0))],
        out_specs=pl.BlockSpec((seq, d), lambda h, b: (b, h)),
        out_shape=jax.ShapeDtypeStruct((m, NA_WIDTH), BF16),
        compiler_params=_params("arbitrary", "arbitrary"),
        name="na_mixer",
    )(proj, proj, proj, bias_table)


def _pack_moves():
    src_off = np.concatenate([[0], np.cumsum((GLA_K_WIDTH, GLA_K_WIDTH, GLA_WIDTH, GLA_WIDTH, GLA_RANK, GLA_RANK,
                                              GMLP_WIDTH, GMLP_WIDTH, NA_WIDTH, NA_WIDTH, NA_WIDTH))]).tolist()
    s_gq, s_gk, s_gv, s_gg, s_af, s_ab, s_mu, s_mv, s_nq, s_nk, s_nv, _ = src_off
    moves = []
    for h in range(GLA_HEADS):
        moves.append((OFF_GQ + h * GLA_DKP, s_gq + h * GLA_DK, GLA_DK))
        moves.append((OFF_GK + h * GLA_DKP, s_gk + h * GLA_DK, GLA_DK))
    moves += [(OFF_MU, s_mu, GMLP_WIDTH), (OFF_GV, s_gv, GLA_WIDTH), (OFF_GG, s_gg, GLA_WIDTH),
              (OFF_NQ, s_nq, NA_WIDTH), (OFF_NK, s_nk, NA_WIDTH), (OFF_NV, s_nv, NA_WIDTH),
              (OFF_A, s_af, 2 * GLA_RANK), (OFF_MV, s_mv, GMLP_WIDTH)]
    return sorted(moves)


def _pack_in_body(w_ref, g_ref, o_ref):
    g = g_ref[...]
    cursor = 0
    for dst, src, width in _pack_moves():
        if dst > cursor:
            o_ref[cursor:dst, :] = jnp.zeros((dst - cursor, o_ref.shape[1]), o_ref.dtype)
        o_ref[dst:dst + width, :] = (w_ref[src:src + width, :] * g).astype(o_ref.dtype)
        cursor = dst + width
    if cursor < o_ref.shape[0]:
        o_ref[cursor:, :] = jnp.zeros((o_ref.shape[0] - cursor, o_ref.shape[1]), o_ref.dtype)


def _pack_w_in(w, g):
    depth, d, n = w.shape
    wt = jnp.swapaxes(w, 1, 2)
    bc = _tile(d, 256)
    return pl.pallas_call(
        _pack_in_body,
        grid=(depth, d // bc),
        in_specs=[pl.BlockSpec((None, n, bc), lambda l, i: (l, 0, i)),
                  pl.BlockSpec((None, 1, bc), lambda l, i: (l, 0, i))],
        out_specs=pl.BlockSpec((None, PROJ_WIDTH, bc), lambda l, i: (l, 0, i)),
        out_shape=jax.ShapeDtypeStruct((depth, PROJ_WIDTH, d), BF16),
        compiler_params=_params("arbitrary", "arbitrary"),
        name="pack_w_in",
    )(wt, g[:, None, :])


def _pack_decay(w2, bias, slot):
    groups = GLA_HEADS // GLA_HPS
    w = jnp.pad(w2.reshape(GLA_RANK, GLA_HEADS, GLA_DK), ((0, 0), (0, 0), (0, GLA_DKP - GLA_DK)))
    w = w.reshape(GLA_RANK, groups, GLA_HPS * GLA_DKP).transpose(1, 0, 2)
    w = jnp.pad(w, ((0, 0), (slot * GLA_RANK, LANE - (slot + 1) * GLA_RANK), (0, 0)))
    b = jnp.pad(bias.reshape(GLA_HEADS, GLA_DK), ((0, 0), (0, GLA_DKP - GLA_DK)))
    return w.astype(BF16), b.reshape(groups, 1, GLA_HPS * GLA_DKP).astype(F32)


def _scaled_bf16(w, g):
    return (w * g[:, :, None]).astype(BF16)


def _cast_pad_body(w_ref, o_ref, *, valid_rows, scale):
    br = o_ref.shape[0]
    row = pl.program_id(0) * br + lax.broadcasted_iota(jnp.int32, (br, 1), 0)
    o_ref[...] = jnp.where(row < valid_rows, w_ref[...] * scale, 0.0).astype(o_ref.dtype)


def _cast_pad_rows(w, layer, rows, scale):
    _, k, n = w.shape
    br = _tile(rows, 512)
    assert rows - k < br
    return pl.pallas_call(
        functools.partial(_cast_pad_body, valid_rows=k, scale=scale),
        grid=(rows // br,),
        in_specs=[pl.BlockSpec((None, br, n), lambda i: (layer, i, 0))],
        out_specs=pl.BlockSpec((None, br, n), lambda i: (0, i, 0)),
        out_shape=jax.ShapeDtypeStruct((1, rows, n), BF16),
        compiler_params=_params("arbitrary"),
        name="cast_pad_rows",
    )(w)


def _mixer(x, xb, ss, batch, seq, layer, w_in_p, w_gate_b, w_a2_f, b_f, w_a2_b, b_b, out_norm,
           ln_g, ln_b, w_s, b_s, rpb, w_branch_b, w_out_b):
    proj = _normed_linear(xb, ss, w_in_p, layer, None, "mixer_in_proj", w_transposed=True)
    gates = _normed_linear(xb, ss, w_gate_b, 0, "sigmoid", "mixer_gates")

    w2f, bias_f = _pack_decay(w_a2_f, b_f, 0)
    w2b, bias_b = _pack_decay(w_a2_b, b_b, 1)
    o_gla = _gla(proj, w2f, w2b, bias_f, bias_b, out_norm, batch, seq)

    gmlp_bias = jnp.repeat(b_s.T, GMLP_GROUP_DIM, axis=1)
    o_gmlp = _gmlp(proj, ln_g, ln_b, w_s.astype(BF16), gmlp_bias)

    o_na = _na(proj, _na_bias_table(rpb, seq // GRID_W), batch, seq)

    y = _merge(o_gla, o_gmlp, o_na, w_branch_b, 0, gates)
    return _residual_linear(y, w_out_b, 0, x, 512, 4096, True, "mixer_out")


def kernel(x, ffn1_norm, ffn1_w_in, ffn1_w_out, mix_norm, w_in, w_gate, gla_w_a2_fwd, gla_b_fwd, gla_w_a2_bwd, gla_b_bwd, gla_out_norm, gmlp_ln_g, gmlp_ln_b, gmlp_w_s, gmlp_b_s, na_rpb, w_branch, w_out, ffn2_norm, ffn2_w_in, ffn2_w_out, final_norm):
    batch, seq, d = x.shape
    depth = ffn1_norm.shape[0]
    assert seq % (NA_QROWS * GRID_W) == 0 and seq // GRID_W >= NA_KROWS and seq % GMLP_CHUNK == 0
    f_pad = _round_up(ffn1_w_out.shape[1], 1024)

    ffns = [w for l in range(depth) for w in ((ffn1_w_in, ffn1_w_out, ffn1_norm, l), (ffn2_w_in, ffn2_w_out, ffn2_norm, l))]

    def in_rider(i):
        w_i, _, norm, l = ffns[i]
        return _Rider(w_i, l, norm[:, :, None], 1.0, w_i.shape[1])

    def out_rider(i):
        return _Rider(ffns[i][1], ffns[i][3], None, 0.5, f_pad)

    def convert_in(i):
        w_i, _, norm, l = ffns[i]
        return _scaled_bf16(w_i[l:l + 1], norm[l:l + 1])

    def convert_out(i):
        return _cast_pad_rows(ffns[i][1], ffns[i][3], f_pad, 0.5)

    def mixer_riders(l):
        return (_Rider(w_gate, l, mix_norm[:, :, None], 1.0, w_gate.shape[1]),
                _Rider(w_branch, l, None, 1.0, w_branch.shape[1]),
                _Rider(w_out, l, None, 1.0, w_out.shape[1]))

    def convert_mixer(l):
        return (_scaled_bf16(w_gate[l:l + 1], mix_norm[l:l + 1]),
                w_branch[l:l + 1].astype(BF16), w_out[l:l + 1].astype(BF16))

    w_in_p = _pack_w_in(w_in, mix_norm)

    xs = x.reshape(batch * seq, d)
    xb, ss = _prenorm(xs)
    w_i, w_o = convert_in(0), convert_out(0)
    mixer_w = None
    for i in range(len(ffns)):
        has_next = i + 1 < len(ffns)
        l = i // 2
        if i % 2 == 1:
            xs, xb, ss = _mixer(xs, xb, ss, batch, seq, l, w_in_p, mixer_w[0], gla_w_a2_fwd[l], gla_b_fwd[l],
                                gla_w_a2_bwd[l], gla_b_bwd[l], gla_out_norm[l], gmlp_ln_g[l], gmlp_ln_b[l],
                                gmlp_w_s[l], gmlp_b_s[l], na_rpb[l], mixer_w[1], mixer_w[2])
        riders = ((in_rider(i + 1), out_rider(i + 1)) if has_next else ()) + (mixer_riders(l) if i % 2 == 0 else ())
        act, converted = _ffn_in(xb, ss, w_i, 0, f_pad, riders)
        outs = _residual_linear(act, w_o, 0, xs, 1024, 2816, has_next, "ffn_out")
        if i % 2 == 0:
            fallback = convert_mixer(l) if None in converted[-3:] else None
            mixer_w = [c[None] if c is not None else fallback[n] for n, c in enumerate(converted[-3:])]
        if has_next:
            xs, xb, ss = outs
            w_i = convert_in(i + 1) if converted[0] is None else converted[0][None]
            w_o = convert_out(i + 1) if converted[1] is None else converted[1][None]
        else:
            xs = outs[0]
    return _rmsnorm(xs, final_norm, F32).reshape(batch, seq, d)
```

```python
import functools
import math
import typing

import numpy as np
import jax
import jax.numpy as jnp
from jax import lax
from jax.experimental import pallas as pl
from jax.experimental.pallas import tpu as pltpu

F32 = jnp.float32
BF16 = jnp.bfloat16
EPS = 1e-6

GRID_W = 64
GLA_HEADS, GLA_DK, GLA_DV, GLA_RANK, GLA_TAU, GLA_CHUNK = 4, 192, 384, 16, 16.0, 64
GLA_DKP = 256
GMLP_GROUPS, GMLP_GROUP_DIM, GMLP_CHUNK = 8, 128, 128
GMLP_WIDTH = GMLP_GROUPS * GMLP_GROUP_DIM
NA_HEADS, NA_HEAD_DIM, NA_WIN_ROWS, NA_WIN_COLS = 12, 128, 8, 16
NA_WIDTH = NA_HEADS * NA_HEAD_DIM
GLA_K_WIDTH = GLA_HEADS * GLA_DK
GLA_WIDTH = GLA_HEADS * GLA_DV
NA_QROWS = 4
NA_KROWS = NA_QROWS + NA_WIN_ROWS

OFF_GQ, OFF_GK, OFF_MU, OFF_GV, OFF_GG = 0, 1024, 2048, 3072, 4608
OFF_NQ, OFF_NK, OFF_NV, OFF_A, OFF_MV = 6144, 7680, 9216, 10752, 11264
PROJ_WIDTH = 12288
LANE = 128

VMEM_LIMIT = 56 * 1024 * 1024

ROW_TILE = 1024
LINEAR_COLS = 1024
FFN_IN_COLS = 512
FFN_OUT_COLS, FFN_OUT_DEPTH = 1024, 2816
MIXER_OUT_COLS = 512
MERGE_COLS = 1024
STREAM_ROWS = 512


def _round_up(x, m):
    return (x + m - 1) // m * m


def _tile(dim, pref):
    if dim <= pref:
        return dim
    t = pref - pref % LANE
    while dim % t:
        t -= LANE
    return t


def _params(*sem):
    return pltpu.CompilerParams(dimension_semantics=sem, vmem_limit_bytes=VMEM_LIMIT)


def _lane_group_sum(x):
    out = x[:, :LANE]
    for c in range(1, x.shape[1] // LANE):
        out = out + x[:, c * LANE:(c + 1) * LANE]
    return out


def _rstd(ss_ref, d):
    return lax.rsqrt(jnp.sum(ss_ref[...], axis=-1, keepdims=True) * (1.0 / d) + EPS)


def _prenorm_body(x_ref, xb_ref, ss_ref):
    x = x_ref[...]
    xb_ref[...] = x.astype(xb_ref.dtype)
    ss_ref[...] = _lane_group_sum(x * x)


def _prenorm(x):
    m, d = x.shape
    bm = _tile(m, STREAM_ROWS)
    return pl.pallas_call(
        _prenorm_body,
        grid=(m // bm,),
        in_specs=[pl.BlockSpec((bm, d), lambda i: (i, 0))],
        out_specs=[pl.BlockSpec((bm, d), lambda i: (i, 0)),
                   pl.BlockSpec((bm, LANE), lambda i: (i, 0))],
        out_shape=[jax.ShapeDtypeStruct((m, d), BF16), jax.ShapeDtypeStruct((m, LANE), F32)],
        compiler_params=_params("arbitrary"),
        name="prenorm",
    )(x)


def _rmsnorm_body(x_ref, g_ref, o_ref):
    x = x_ref[...]
    ms = jnp.mean(x * x, axis=-1, keepdims=True)
    o_ref[...] = (x * lax.rsqrt(ms + EPS) * g_ref[...]).astype(o_ref.dtype)


def _rmsnorm(x, g, out_dtype):
    m, d = x.shape
    bm = _tile(m, STREAM_ROWS)
    return pl.pallas_call(
        _rmsnorm_body,
        grid=(m // bm,),
        in_specs=[pl.BlockSpec((bm, d), lambda i: (i, 0)),
                  pl.BlockSpec((1, d), lambda i: (0, 0))],
        out_specs=pl.BlockSpec((bm, d), lambda i: (i, 0)),
        out_shape=jax.ShapeDtypeStruct((m, d), out_dtype),
        compiler_params=_params("arbitrary"),
        name="rmsnorm",
    )(x, g.reshape(1, d))


def _normed_linear_body(x_ref, ss_ref, w_ref, o_ref, *, act, w_transposed):
    contract = (((1,), (1 if w_transposed else 0,)), ((), ()))
    acc = _rstd(ss_ref, x_ref.shape[1]) * lax.dot_general(x_ref[...], w_ref[...], contract,
                                                          preferred_element_type=F32)
    if act == "sigmoid":
        acc = _sigmoid(acc)
    o_ref[...] = acc.astype(o_ref.dtype)


def _normed_linear(xb, ss, w, layer, act, name, w_transposed=False):
    m, k = xb.shape
    n = w.shape[1] if w_transposed else w.shape[2]
    bm, bn = _tile(m, ROW_TILE), _tile(n, LINEAR_COLS)
    w_spec =(pl.BlockSpec((None, bn, k), lambda i, j: (layer, j, 0)) if w_transposed
              else pl.BlockSpec((None, k, bn), lambda i, j: (layer, 0, j)))
    return pl.pallas_call(
        functools.partial(_normed_linear_body, act=act, w_transposed=w_transposed),
        grid=(m // bm, n // bn),
        in_specs=[pl.BlockSpec((bm, k), lambda i, j: (i, 0)),
                  pl.BlockSpec((bm, ss.shape[1]), lambda i, j: (i, 0)),
                  w_spec],
        out_specs=pl.BlockSpec((bm, bn), lambda i, j: (i, j)),
        out_shape=jax.ShapeDtypeStruct((m, n), BF16),
        compiler_params=_params("arbitrary", "arbitrary"),
        name=name,
    )(xb, ss, w)


RIDER_BLOCK_BYTES = 3 * 1024 * 1024


class _Rider(typing.NamedTuple):
    src: jax.Array
    layer: int
    gain: typing.Optional[jax.Array]
    scale: float
    out_rows: int


def _rider_rows(rider, steps):
    _, r, c = rider.src.shape
    for rr in range(16, min(RIDER_BLOCK_BYTES // (4 * c), r) + 1, 16):
        if r % rr == 0 and rider.out_rows % rr == 0 and rider.out_rows // rr <= steps:
            return rr
    return None


def _rider_specs(rider, rr, step_of):
    _, r, c = rider.src.shape
    n_valid, n_out = r // rr, rider.out_rows // rr
    dst_idx = lambda *g: jnp.minimum(step_of(*g), n_out - 1)
    src_map = lambda *g: (rider.layer, jnp.minimum(dst_idx(*g), n_valid - 1), 0)
    in_specs, in_args = [pl.BlockSpec((None, rr, c), src_map)], [rider.src]
    if rider.gain is not None:
        in_specs.append(pl.BlockSpec((None, rr, 1), src_map))
        in_args.append(rider.gain)
    out_spec = pl.BlockSpec((rr, c), lambda *g: (dst_idx(*g), 0))
    return in_specs, in_args, out_spec, jax.ShapeDtypeStruct((rider.out_rows, c), BF16)


def _rider_step(refs, step, n_valid, n_out, scale):
    src_ref, dst_ref = refs[0], refs[-1]
    v = src_ref[...]
    if len(refs) == 3:
        v = v * refs[1][...]
    if scale != 1.0:
        v = v * scale
    if n_out > n_valid:
        v = jnp.where(jnp.minimum(step, n_out - 1) < n_valid, v, 0.0)
    dst_ref[...] = v.astype(dst_ref.dtype)


def _sigmoid(x):
    return 0.5 * jnp.tanh(0.5 * x) + 0.5


def _ffn_in_body(x_ref, ss_ref, wa_ref, wb0_ref, wb1_ref, *refs, width, tail, riders):
    n_rider_in = sum(meta[0] for meta in riders)
    o_ref = refs[n_rider_in]
    bn = o_ref.shape[1]
    half = bn // 2

    def ride():
        step = pl.program_id(0) * pl.num_programs(1) + pl.program_id(1)
        first = 0
        for r, meta in enumerate(riders):
            _rider_step(refs[first:first + meta[0]] + (refs[n_rider_in + 1 + r],), step, *meta[1:])
            first += meta[0]

    def compute(valid):
        ride()
        x = x_ref[...]
        rstd = _rstd(ss_ref, x_ref.shape[1])
        a = rstd * jnp.dot(x, wa_ref[:, :valid], preferred_element_type=F32)
        for c, wb_ref in enumerate((wb0_ref, wb1_ref)[:valid // half]):
            cols = slice(c * half, (c + 1) * half)
            b = rstd * jnp.dot(x, wb_ref[...], preferred_element_type=F32)
            ac = a[:, cols]
            o_ref[:, cols] = (ac * _sigmoid(ac) * b).astype(o_ref.dtype)
        if valid < bn:
            o_ref[:, valid:] = jnp.zeros((o_ref.shape[0], bn - valid), o_ref.dtype)

    if tail == bn:
        compute(bn)
    else:
        is_tail = (pl.program_id(1) + 1) * bn > width
        pl.when(jnp.logical_not(is_tail))(lambda: compute(bn))
        pl.when(is_tail)(lambda: compute(tail))


def _ffn_in(xb, ss, w, layer, f_pad, riders=()):
    m, k = xb.shape
    f = w.shape[2] // 2
    bm, bn = _tile(m, ROW_TILE), _tile(f_pad, FFN_IN_COLS)
    half = bn // 2
    nb = f // half
    last = 2 * nb - 1
    tail = f - (f_pad // bn - 1) * bn
    assert f % half == 0 and tail in (half, bn)
    grid = (m // bm, f_pad // bn)
    in_specs = [pl.BlockSpec((bm, k), lambda i, j: (i, 0)),
                pl.BlockSpec((bm, ss.shape[1]), lambda i, j: (i, 0)),
                pl.BlockSpec((None, k, bn), lambda i, j: (layer, 0, j)),
                pl.BlockSpec((None, k, half), lambda i, j: (layer, 0, jnp.minimum(nb + 2 * j, last))),
                pl.BlockSpec((None, k, half), lambda i, j: (layer, 0, jnp.minimum(nb + 2 * j + 1, last)))]
    args = [xb, ss, w, w, w]
    out_specs = [pl.BlockSpec((bm, bn), lambda i, j: (i, j))]
    out_shape = [jax.ShapeDtypeStruct((m, f_pad), BF16)]
    metas, hosted = [], []
    for rider in riders:
        rr = _rider_rows(rider, grid[0] * grid[1])
        hosted.append(rr is not None)
        if rr is None:
            continue
        r_in, r_args, r_out, r_shape = _rider_specs(rider, rr, lambda i, j: i * grid[1] + j)
        in_specs += r_in
        args += r_args
        out_specs.append(r_out)
        out_shape.append(r_shape)
        metas.append((len(r_in), rider.src.shape[1] // rr, rider.out_rows // rr, rider.scale))
    out = pl.pallas_call(
        functools.partial(_ffn_in_body, width=f, tail=tail, riders=tuple(metas)),
        grid=grid,
        in_specs=in_specs,
        out_specs=out_specs,
        out_shape=out_shape,
        compiler_params=_params("arbitrary", "arbitrary"),
        name="ffn_in",
    )(*args)
    converted = iter(out[1:])
    return out[0], [next(converted) if h else None for h in hosted]


def _residual_body(a_ref, w_ref, x_ref, o_ref, *norm_refs, tail_k):
    nk = pl.num_programs(2)

    @pl.when(pl.program_id(2) == 0)
    def _():
        o_ref[...] = x_ref[...]

    def accumulate(depth):
        o_ref[...] += jnp.dot(a_ref[:, :depth], w_ref[:depth, :], preferred_element_type=F32)

    if tail_k == a_ref.shape[1]:
        accumulate(tail_k)
    else:
        pl.when(pl.program_id(2) < nk - 1)(lambda: accumulate(a_ref.shape[1]))
        pl.when(pl.program_id(2) == nk - 1)(lambda: accumulate(tail_k))

    if norm_refs:
        xb_ref, ss_ref = norm_refs

        @pl.when(pl.program_id(2) == nk - 1)
        def _():
            xn = o_ref[...]
            xb_ref[...] = xn.astype(xb_ref.dtype)
            part = _lane_group_sum(xn * xn)

            @pl.when(pl.program_id(1) == 0)
            def _():
                ss_ref[...] = part

            @pl.when(pl.program_id(1) > 0)
            def _():
                ss_ref[...] += part


def _residual_linear(a, w, layer, x, bn_pref, tk_pref, emit_norm, name, k_valid=None):
    m, k = a.shape
    n = w.shape[2]
    bm, bn, tk = _tile(m, ROW_TILE), _tile(n, bn_pref), _tile(k, tk_pref)
    tail_k = _round_up((k if k_valid is None else k_valid) - (k // tk - 1) * tk, 2 * LANE)
    assert 0 < tail_k <= tk
    out_specs = [pl.BlockSpec((bm, bn), lambda i, j, kk: (i, j))]
    out_shape = [jax.ShapeDtypeStruct((m, n), F32)]
    if emit_norm:
        out_specs += [pl.BlockSpec((bm, bn), lambda i, j, kk: (i, j)),
                      pl.BlockSpec((bm, LANE), lambda i, j, kk: (i, 0))]
        out_shape += [jax.ShapeDtypeStruct((m, n), BF16), jax.ShapeDtypeStruct((m, LANE), F32)]
    return pl.pallas_call(
        functools.partial(_residual_body, tail_k=tail_k),
        grid=(m // bm, n // bn, k // tk),
        in_specs=[pl.BlockSpec((bm, tk), lambda i, j, kk: (i, kk)),
                  pl.BlockSpec((None, tk, bn), lambda i, j, kk: (layer, kk, j)),
                  pl.BlockSpec((bm, bn), lambda i, j, kk: (i, j))],
        out_specs=out_specs,
        out_shape=out_shape,
        compiler_params=_params("arbitrary", "arbitrary", "arbitrary"),
        name=name,
    )(a, w, x)


MERGE_ROWS = 512


def _merge_body(og_ref, om_ref, on_ref, *refs):
    nblk = (og_ref.shape[1] + om_ref.shape[1] + on_ref.shape[1]) // MERGE_ROWS
    w_refs, (g1_ref, g2_ref, g3_ref, y_ref) = refs[:nblk], refs[nblk:]
    y = None
    first = 0
    for o_ref, g_ref in ((og_ref, g1_ref), (om_ref, g2_ref), (on_ref, g3_ref)):
        cnt = o_ref.shape[1] // MERGE_ROWS
        w = jnp.concatenate([r[...] for r in w_refs[first:first + cnt]], axis=0)
        first += cnt
        term = g_ref[...].astype(F32) * jnp.dot(o_ref[...], w, preferred_element_type=F32)
        y = term if y is None else y + term
    y_ref[...] = y.astype(y_ref.dtype)


def _merge(o_gla, o_gmlp, o_na, w_branch, layer, gates):
    m = o_gla.shape[0]
    kk, d = w_branch.shape[1:]
    bm, bn = _tile(m, ROW_TILE), _tile(d, MERGE_COLS)
    nb = d // bn
    row = lambda i, j: (i, 0)
    w_specs = [pl.BlockSpec((None, MERGE_ROWS, bn), functools.partial(lambda i, j, r: (layer, r, j), r=r))
               for r in range(kk // MERGE_ROWS)]
    return pl.pallas_call(
        _merge_body,
        grid=(m // bm, nb),
        in_specs=[pl.BlockSpec((bm, o_gla.shape[1]), row),
                  pl.BlockSpec((bm, o_gmlp.shape[1]), row),
                  pl.BlockSpec((bm, o_na.shape[1]), row)] + w_specs +
                 [pl.BlockSpec((bm, bn), lambda i, j: (i, j)),
                  pl.BlockSpec((bm, bn), lambda i, j: (i, j + nb)),
                  pl.BlockSpec((bm, bn), lambda i, j: (i, j + 2 * nb))],
        out_specs=pl.BlockSpec((bm, bn), lambda i, j: (i, j)),
        out_shape=jax.ShapeDtypeStruct((m, d), BF16),
        compiler_params=_params("arbitrary", "arbitrary"),
        name="branch_merge",
    )(o_gla, o_gmlp, o_na, *([w_branch] * (kk // MERGE_ROWS)), gates, gates, gates)


GLA_ROWS = 256
GLA_HPS = 2


def _log_sigmoid(x):
    return jnp.minimum(x, 0.0) - jnp.log(1.0 + jnp.exp(-jnp.abs(x)))


def _dot_exact01(tri, x):
    hi = x.astype(BF16)
    rest = x - hi.astype(F32)
    mid = rest.astype(BF16)
    lo = (rest - mid.astype(F32)).astype(BF16)
    return (jnp.dot(tri, hi, preferred_element_type=F32) + jnp.dot(tri, mid, preferred_element_type=F32)
            + jnp.dot(tri, lo, preferred_element_type=F32))


def _gla_body(q_ref, k_ref, v_ref, g_ref, a_ref, w2f_ref, w2b_ref, bf_ref, bb_ref, gn_ref, o_ref,
              cf_s, cb_s, acc_s, sf_s, sb_s):
    seq = q_ref.shape[0]
    c = GLA_CHUNK
    kp, dv = GLA_DKP, GLA_DV
    n_chunks = seq // c
    rb = min(GLA_ROWS, seq)

    r = lax.broadcasted_iota(jnp.int32, (rb, rb), 0)
    cc = lax.broadcasted_iota(jnp.int32, (rb, rb), 1)
    same = lax.shift_right_logical(r, 6) == lax.shift_right_logical(cc, 6)
    tri_f = jnp.where(same & (cc <= r), 1.0, 0.0).astype(BF16)
    tri_b = jnp.where(same & (cc >= r), 1.0, 0.0).astype(BF16)

    def decay_step(i, carry):
        rows = pl.ds(pl.multiple_of(i * rb, rb), rb)
        a = a_ref[rows, :]
        la_f = _log_sigmoid(jnp.dot(a, w2f_ref[0], preferred_element_type=F32) + bf_ref[0]) * (1.0 / GLA_TAU)
        la_b = _log_sigmoid(jnp.dot(a, w2b_ref[0], preferred_element_type=F32) + bb_ref[0]) * (1.0 / GLA_TAU)
        cf_s[rows, :] = _dot_exact01(tri_f, la_f)
        cb_s[rows, :] = _dot_exact01(tri_b, la_b)
        acc_s[rows, :] = jnp.zeros((rb, acc_s.shape[1]), F32)
        return carry

    lax.fori_loop(0, seq // rb, decay_step, 0)
    sf_s[...] = jnp.zeros(sf_s.shape, F32)
    sb_s[...] = jnp.zeros(sb_s.shape, F32)

    ii = lax.broadcasted_iota(jnp.int32, (c, c), 0)
    jj = lax.broadcasted_iota(jnp.int32, (c, c), 1)
    mask_f = jj <= ii
    mask_b = jj > ii
    q_scale = GLA_DK ** -0.5

    def chunk(idx, hh, cum_s, state_s, mask, mid, last):
        rows = pl.ds(pl.multiple_of(idx * c, c), c)
        kcols = slice(hh * kp, (hh + 1) * kp)
        vcols = slice(hh * dv, (hh + 1) * dv)
        cum = cum_s[rows, kcols]
        b_mid = cum[mid:mid + 1, :]
        b_last = cum[last:last + 1, :]
        q = q_ref[rows, kcols].astype(F32) * q_scale
        k = k_ref[rows, kcols].astype(F32)
        v = v_ref[rows, vcols]
        q_intra = (q * jnp.exp(cum - b_mid)).astype(BF16)
        k_intra = (k * jnp.exp(b_mid - cum)).astype(BF16)
        scores = lax.dot_general(q_intra, k_intra, (((1,), (1,)), ((), ())), preferred_element_type=F32)
        scores = jnp.where(mask, scores, 0.0).astype(BF16)
        o = jnp.dot(scores, v, preferred_element_type=F32)
        state = state_s[hh]
        q_inter = (q * jnp.exp(cum)).astype(BF16)
        o += lax.dot_general(q_inter, state.astype(BF16), (((1,), (1,)), ((), ())), preferred_element_type=F32)
        k_state = (k * jnp.exp(b_last - cum)).astype(BF16)
        kv = lax.dot_general(v, k_state, (((0,), (0,)), ((), ())), preferred_element_type=F32)
        state_s[hh] = state * jnp.exp(b_last) + kv
        acc_s[rows, vcols] += o

    def scan_step(n, carry):
        for hh in range(sf_s.shape[0]):
            chunk(n, hh, cf_s, sf_s, mask_f, c // 2 - 1, c - 1)
            chunk(n_chunks - 1 - n, hh, cb_s, sb_s, mask_b, c // 2, 0)
        return carry

    lax.fori_loop(0, n_chunks, scan_step, 0, unroll=8)

    def out_step(i, carry):
        rows = pl.ds(pl.multiple_of(i * rb, rb), rb)
        for hh in range(sf_s.shape[0]):
            vcols = slice(hh * dv, (hh + 1) * dv)
            o = acc_s[rows, vcols]
            ms = jnp.mean(o * o, axis=-1, keepdims=True)
            g = g_ref[rows, vcols].astype(F32)
            o_ref[rows, vcols] = (o * lax.rsqrt(ms + EPS) * gn_ref[...] * (g * _sigmoid(g))).astype(o_ref.dtype)
        return carry

    lax.fori_loop(0, seq // rb, out_step, 0)


def _gla(proj, w2f, w2b, bias_f, bias_b, out_norm, batch, seq):
    m = proj.shape[0]
    hps = GLA_HPS
    kb, vb = hps * GLA_DKP, hps * GLA_DV
    assert OFF_GQ % kb == 0 and OFF_GK % kb == 0 and OFF_GV % vb == 0 and OFF_GG % vb == 0
    return pl.pallas_call(
        _gla_body,
        grid=(batch, GLA_HEADS // hps),
        in_specs=[pl.BlockSpec((seq, kb), lambda b, h: (b, OFF_GQ // kb + h)),
                  pl.BlockSpec((seq, kb), lambda b, h: (b, OFF_GK // kb + h)),
                  pl.BlockSpec((seq, vb), lambda b, h: (b, OFF_GV // vb + h)),
                  pl.BlockSpec((seq, vb), lambda b, h: (b, OFF_GG // vb + h)),
                  pl.BlockSpec((seq, LANE), lambda b, h: (b, OFF_A // LANE)),
                  pl.BlockSpec((1, LANE, kb), lambda b, h: (h, 0, 0)),
                  pl.BlockSpec((1, LANE, kb), lambda b, h: (h, 0, 0)),
                  pl.BlockSpec((1, 1, kb), lambda b, h: (h, 0, 0)),
                  pl.BlockSpec((1, 1, kb), lambda b, h: (h, 0, 0)),
                  pl.BlockSpec((1, GLA_DV), lambda b, h: (0, 0))],
        out_specs=pl.BlockSpec((seq, vb), lambda b, h: (b, h)),
        out_shape=jax.ShapeDtypeStruct((m, GLA_WIDTH), BF16),
        scratch_shapes=[pltpu.VMEM((seq, kb), F32), pltpu.VMEM((seq, kb), F32),
                        pltpu.VMEM((seq, vb), F32),
                        pltpu.VMEM((hps, GLA_DV, GLA_DKP), F32), pltpu.VMEM((hps, GLA_DV, GLA_DKP), F32)],
        compiler_params=_params("arbitrary", "arbitrary"),
        name="gla_mixer",
    )(proj, proj, proj, proj, proj, w2f, w2b, bias_f, bias_b, out_norm.reshape(1, GLA_DV))


GMLP_ROWS = 512


def _gelu_tanh(x):
    return 0.5 * x * (1.0 + jnp.tanh(math.sqrt(2.0 / math.pi) * (x + 0.044715 * (x * x * x))))


def _gmlp_body(u_ref, v_ref, lg_ref, lb_ref, ws_ref, bs_ref, o_ref):
    t = GMLP_CHUNK
    for ci in range(u_ref.shape[0] // t):
        rows = slice(ci * t, (ci + 1) * t)
        vf = _gelu_tanh(v_ref[rows, :].astype(F32))
        mu = jnp.mean(vf, axis=-1, keepdims=True)
        var = jnp.mean(jnp.square(vf - mu), axis=-1, keepdims=True)
        vn = ((vf - mu) * lax.rsqrt(var + EPS) * lg_ref[...] + lb_ref[...]).astype(BF16)
        for g in range(GMLP_GROUPS):
            cols = slice(g * GMLP_GROUP_DIM, (g + 1) * GMLP_GROUP_DIM)
            s = jnp.dot(ws_ref[g], vn[:, cols], preferred_element_type=F32) + bs_ref[:, cols]
            u = _gelu_tanh(u_ref[rows, cols].astype(F32))
            o_ref[rows, cols] = (u * s).astype(o_ref.dtype)


def _gmlp(proj, ln_g, ln_b, w_s, bias):
    m = proj.shape[0]
    w = GMLP_WIDTH
    bm = _tile(m, GMLP_ROWS)
    return pl.pallas_call(
        _gmlp_body,
        grid=(m // bm,),
        in_specs=[pl.BlockSpec((bm, w), lambda i: (i, OFF_MU // w)),
                  pl.BlockSpec((bm, w), lambda i: (i, OFF_MV // w)),
                  pl.BlockSpec((1, w), lambda i: (0, 0)),
                  pl.BlockSpec((1, w), lambda i: (0, 0)),
                  pl.BlockSpec((GMLP_GROUPS, GMLP_CHUNK, GMLP_CHUNK), lambda i: (0, 0, 0)),
                  pl.BlockSpec((GMLP_CHUNK, w), lambda i: (0, 0))],
        out_specs=pl.BlockSpec((bm, w), lambda i: (i, 0)),
        out_shape=jax.ShapeDtypeStruct((m, w), BF16),
        compiler_params=_params("arbitrary"),
        name="gmlp_mixer",
    )(proj, proj, ln_g.reshape(1, w), ln_b.reshape(1, w), w_s, bias)


def _na_key_start(qb, rows):
    return int(np.clip(qb * NA_QROWS - NA_WIN_ROWS // 2, 0, rows - NA_KROWS))


def _na_table_index(qb, nqb):
    return 0 if qb == 0 else (2 if qb == nqb - 1 else 1)


def _na_table_plan(rows):
    nqb = rows // NA_QROWS
    plan = []
    for qb in (0, 1, nqb - 1):
        start = _na_key_start(qb, rows)
        per_row = []
        for a in range(NA_QROWS):
            qr = qb * NA_QROWS + a
            r0 = int(np.clip(qr - NA_WIN_ROWS // 2, 0, rows - NA_WIN_ROWS))
            kr = start + np.arange(NA_KROWS)
            per_row.append((start - qr + NA_WIN_ROWS - 1, (kr >= r0) & (kr < r0 + NA_WIN_ROWS)))
        plan.append(per_row)
    return plan


def _na_body(q_ref, k_ref, v_ref, pair_ref, mask_ref, o_ref, tab_s, *, rows):
    nqb = rows // NA_QROWS
    qn = NA_QROWS * GRID_W
    d = NA_HEAD_DIM
    heads = range(q_ref.shape[1] // d)

    @pl.when(pl.program_id(1) == 0)
    def _():
        for hh in heads:
            for t, per_row in enumerate(_na_table_plan(rows)):
                for a, (dy0, ok) in enumerate(per_row):
                    for p in range(NA_KROWS // 2):
                        case = 2 * int(ok[2 * p]) + int(ok[2 * p + 1])
                        tab_s[hh, t, a * GRID_W:(a + 1) * GRID_W, p * LANE:(p + 1) * LANE] = (
                            pair_ref[hh, NA_KROWS + dy0 + 2 * p] + mask_ref[case])

    for qb in range(nqb):
        start = _na_key_start(qb, rows) * GRID_W
        keys = slice(start, start + NA_KROWS * GRID_W)
        qrows = slice(qb * qn, (qb + 1) * qn)
        for hh in heads:
            cols = slice(hh * d, (hh + 1) * d)
            s = lax.dot_general(q_ref[qrows, cols], k_ref[keys, cols], (((1,), (1,)), ((), ())),
                                preferred_element_type=F32)
            s = s * (d ** -0.5) + tab_s[hh, _na_table_index(qb, nqb)]
            p = jnp.exp(s - jnp.max(s, axis=-1, keepdims=True))
            l = jnp.sum(p, axis=-1, keepdims=True)
            o = jnp.dot(p.astype(BF16), v_ref[keys, cols], preferred_element_type=F32)
            o_ref[qrows, cols] = (o / l).astype(o_ref.dtype)


def _na_toeplitz(rpb):
    h, ny, nx = rpb.shape
    w = GRID_W
    left = w - NA_WIN_COLS
    v = jnp.pad(rpb, ((0, 0), (0, 0), (left, 2 * w - left - nx)))
    flat = jnp.broadcast_to(v[:, :, None, :], (h, ny, w, 2 * w)).reshape(h, ny, 2 * w * w)
    skew = flat[:, :, : w * (2 * w - 1)].reshape(h, ny, w, 2 * w - 1)
    return skew[..., w - 1:]


def _na_bias_pairs(rpb):
    toe = jnp.pad(_na_toeplitz(rpb), ((0, 0), (NA_KROWS, NA_KROWS), (0, 0), (0, 0)))
    return jnp.concatenate([toe[:, :-1], toe[:, 1:]], axis=-1)


def _na_masks():
    qc = np.arange(GRID_W)[:, None]
    kc = np.arange(GRID_W)[None, :]
    c0 = np.clip(qc - NA_WIN_COLS // 2, 0, GRID_W - NA_WIN_COLS)
    col_ok = (kc >= c0) & (kc < c0 + NA_WIN_COLS)
    out = np.full((4, GRID_W, 2 * GRID_W), -np.inf, np.float32)
    for case in range(4):
        for half, valid in enumerate((case >> 1, case & 1)):
            if valid:
                out[case, :, half * GRID_W:(half + 1) * GRID_W] = np.where(col_ok, 0.0, -np.inf)
    return jnp.asarray(out)


NA_HPS = 2


def _na(proj, rpb, batch, seq):
    m = proj.shape[0]
    rows = seq // GRID_W
    qn = NA_QROWS * GRID_W
    kn = NA_KROWS * GRID_W
    d = NA_HPS * NA_HEAD_DIM
    assert NA_HEADS % NA_HPS == 0 and OFF_NQ % d == 0 and OFF_NK % d == 0 and OFF_NV % d == 0
    pairs = _na_bias_pairs(rpb)
    return pl.pallas_call(
        functools.partial(_na_body, rows=rows),
        grid=(NA_HEADS // NA_HPS, batch),
        in_specs=[pl.BlockSpec((seq, d), lambda h, b: (b, OFF_NQ // d + h)),
                  pl.BlockSpec((seq, d), lambda h, b: (b, OFF_NK // d + h)),
                  pl.BlockSpec((seq, d), lambda h, b: (b, OFF_NV // d + h)),
                  pl.BlockSpec((NA_HPS,) + pairs.shape[1:], lambda h, b: (h, 0, 0, 0)),
                  pl.BlockSpec((4, GRID_W, 2 * GRID_W), lambda h, b: (0, 0, 0))],
        out_specs=pl.BlockSpec((seq, d), lambda h, b: (b, h)),
        out_shape=jax.ShapeDtypeStruct((m, NA_WIDTH), BF16),
        scratch_shapes=[pltpu.VMEM((NA_HPS, 3, qn, kn), F32)],
        compiler_params=_params("arbitrary", "arbitrary"),
        name="na_mixer",
    )(proj, proj, proj, pairs, _na_masks())


def _pack_moves():
    src_off = np.concatenate([[0], np.cumsum((GLA_K_WIDTH, GLA_K_WIDTH, GLA_WIDTH, GLA_WIDTH, GLA_RANK, GLA_RANK,
                                              GMLP_WIDTH, GMLP_WIDTH, NA_WIDTH, NA_WIDTH, NA_WIDTH))]).tolist()
    s_gq, s_gk, s_gv, s_gg, s_af, s_ab, s_mu, s_mv, s_nq, s_nk, s_nv, _ = src_off
    moves = []
    for h in range(GLA_HEADS):
        moves.append((OFF_GQ + h * GLA_DKP, s_gq + h * GLA_DK, GLA_DK))
        moves.append((OFF_GK + h * GLA_DKP, s_gk + h * GLA_DK, GLA_DK))
    moves += [(OFF_MU, s_mu, GMLP_WIDTH), (OFF_GV, s_gv, GLA_WIDTH), (OFF_GG, s_gg, GLA_WIDTH),
              (OFF_NQ, s_nq, NA_WIDTH), (OFF_NK, s_nk, NA_WIDTH), (OFF_NV, s_nv, NA_WIDTH),
              (OFF_A, s_af, 2 * GLA_RANK), (OFF_MV, s_mv, GMLP_WIDTH)]
    return sorted(moves)


def _pack_in_body(w_ref, g_ref, o_ref):
    g = g_ref[...]
    cursor = 0
    for dst, src, width in _pack_moves():
        if dst > cursor:
            o_ref[cursor:dst, :] = jnp.zeros((dst - cursor, o_ref.shape[1]), o_ref.dtype)
        o_ref[dst:dst + width, :] = (w_ref[src:src + width, :] * g).astype(o_ref.dtype)
        cursor = dst + width
    if cursor < o_ref.shape[0]:
        o_ref[cursor:, :] = jnp.zeros((o_ref.shape[0] - cursor, o_ref.shape[1]), o_ref.dtype)


def _pack_w_in(w, g):
    depth, d, n = w.shape
    wt = jnp.swapaxes(w, 1, 2)
    bc = _tile(d, 256)
    return pl.pallas_call(
        _pack_in_body,
        grid=(depth, d // bc),
        in_specs=[pl.BlockSpec((None, n, bc), lambda l, i: (l, 0, i)),
                  pl.BlockSpec((None, 1, bc), lambda l, i: (l, 0, i))],
        out_specs=pl.BlockSpec((None, PROJ_WIDTH, bc), lambda l, i: (l, 0, i)),
        out_shape=jax.ShapeDtypeStruct((depth, PROJ_WIDTH, d), BF16),
        compiler_params=_params("arbitrary", "arbitrary"),
        name="pack_w_in",
    )(wt, g[:, None, :])


def _pack_decay(w2, bias, slot):
    groups = GLA_HEADS // GLA_HPS
    w = jnp.pad(w2.reshape(GLA_RANK, GLA_HEADS, GLA_DK), ((0, 0), (0, 0), (0, GLA_DKP - GLA_DK)))
    w = w.reshape(GLA_RANK, groups, GLA_HPS * GLA_DKP).transpose(1, 0, 2)
    w = jnp.pad(w, ((0, 0), (slot * GLA_RANK, LANE - (slot + 1) * GLA_RANK), (0, 0)))
    b = jnp.pad(bias.reshape(GLA_HEADS, GLA_DK), ((0, 0), (0, GLA_DKP - GLA_DK)))
    return w.astype(BF16), b.reshape(groups, 1, GLA_HPS * GLA_DKP).astype(F32)


def _scaled_bf16(w, g):
    return (w * g[:, :, None]).astype(BF16)


def _cast_pad_body(w_ref, o_ref, *, valid_rows, scale):
    br = o_ref.shape[0]
    row = pl.program_id(0) * br + lax.broadcasted_iota(jnp.int32, (br, 1), 0)
    o_ref[...] = jnp.where(row < valid_rows, w_ref[...] * scale, 0.0).astype(o_ref.dtype)


def _cast_pad_rows(w, layer, rows, scale):
    _, k, n = w.shape
    br = _tile(rows, 512)
    assert rows - k < br
    return pl.pallas_call(
        functools.partial(_cast_pad_body, valid_rows=k, scale=scale),
        grid=(rows // br,),
        in_specs=[pl.BlockSpec((None, br, n), lambda i: (layer, i, 0))],
        out_specs=pl.BlockSpec((None, br, n), lambda i: (0, i, 0)),
        out_shape=jax.ShapeDtypeStruct((1, rows, n), BF16),
        compiler_params=_params("arbitrary"),
        name="cast_pad_rows",
    )(w)


def _mixer(x, xb, ss, batch, seq, layer, w_in_p, w_gate_b, w_a2_f, b_f, w_a2_b, b_b, out_norm,
           ln_g, ln_b, w_s, b_s, rpb, w_branch_b, w_out_b):
    proj = _normed_linear(xb, ss, w_in_p, layer, None, "mixer_in_proj", w_transposed=True)
    gates = _normed_linear(xb, ss, w_gate_b, 0, "sigmoid", "mixer_gates")

    w2f, bias_f = _pack_decay(w_a2_f, b_f, 0)
    w2b, bias_b = _pack_decay(w_a2_b, b_b, 1)
    o_gla = _gla(proj, w2f, w2b, bias_f, bias_b, out_norm, batch, seq)

    gmlp_bias = jnp.repeat(b_s.T, GMLP_GROUP_DIM, axis=1)
    o_gmlp = _gmlp(proj, ln_g, ln_b, w_s.astype(BF16), gmlp_bias)

    o_na = _na(proj, rpb, batch, seq)

    y = _merge(o_gla, o_gmlp, o_na, w_branch_b, 0, gates)
    return _residual_linear(y, w_out_b, 0, x, MIXER_OUT_COLS, y.shape[1], True, "mixer_out")


def kernel(x, ffn1_norm, ffn1_w_in, ffn1_w_out, mix_norm, w_in, w_gate, gla_w_a2_fwd, gla_b_fwd, gla_w_a2_bwd, gla_b_bwd, gla_out_norm, gmlp_ln_g, gmlp_ln_b, gmlp_w_s, gmlp_b_s, na_rpb, w_branch, w_out, ffn2_norm, ffn2_w_in, ffn2_w_out, final_norm):
    batch, seq, d = x.shape
    depth = ffn1_norm.shape[0]
    assert seq % (NA_QROWS * GRID_W) == 0 and seq // GRID_W >= NA_KROWS and seq % GMLP_CHUNK == 0
    f_pad = _round_up(ffn1_w_out.shape[1], 1024)

    ffns = [w for l in range(depth) for w in ((ffn1_w_in, ffn1_w_out, ffn1_norm, l), (ffn2_w_in, ffn2_w_out, ffn2_norm, l))]

    def in_rider(i):
        w_i, _, norm, l = ffns[i]
        return _Rider(w_i, l, norm[:, :, None], 1.0, w_i.shape[1])

    def out_rider(i):
        return _Rider(ffns[i][1], ffns[i][3], None, 0.5, f_pad)

    def convert_in(i):
        w_i, _, norm, l = ffns[i]
        return _scaled_bf16(w_i[l:l + 1], norm[l:l + 1])

    def convert_out(i):
        return _cast_pad_rows(ffns[i][1], ffns[i][3], f_pad, 0.5)

    def mixer_riders(l):
        return (_Rider(w_gate, l, mix_norm[:, :, None], 1.0, w_gate.shape[1]),
                _Rider(w_branch, l, None, 1.0, w_branch.shape[1]),
                _Rider(w_out, l, None, 1.0, w_out.shape[1]))

    def convert_mixer(l):
        return (_scaled_bf16(w_gate[l:l + 1], mix_norm[l:l + 1]),
                w_branch[l:l + 1].astype(BF16), w_out[l:l + 1].astype(BF16))

    w_in_p = _pack_w_in(w_in, mix_norm)

    xs = x.reshape(batch * seq, d)
    xb, ss = _prenorm(xs)
    w_i, w_o = convert_in(0), convert_out(0)
    mixer_w = None
    for i in range(len(ffns)):
        has_next = i + 1 < len(ffns)
        l = i // 2
        if i % 2 == 1:
            xs, xb, ss = _mixer(xs, xb, ss, batch, seq, l, w_in_p, mixer_w[0], gla_w_a2_fwd[l], gla_b_fwd[l],
                                gla_w_a2_bwd[l], gla_b_bwd[l], gla_out_norm[l], gmlp_ln_g[l], gmlp_ln_b[l],
                                gmlp_w_s[l], gmlp_b_s[l], na_rpb[l], mixer_w[1], mixer_w[2])
        riders = ((in_rider(i + 1), out_rider(i + 1)) if has_next else ()) + (mixer_riders(l) if i % 2 == 0 else ())
        act, converted = _ffn_in(xb, ss, w_i, 0, f_pad, riders)
        outs = _residual_linear(act, w_o, 0, xs, FFN_OUT_COLS, FFN_OUT_DEPTH, has_next, "ffn_out",
                                k_valid=ffns[i][1].shape[1])
        if i % 2 == 0:
            fallback = convert_mixer(l) if None in converted[-3:] else None
            mixer_w = [c[None] if c is not None else fallback[n] for n, c in enumerate(converted[-3:])]
        if has_next:
            xs, xb, ss = outs
            w_i = convert_in(i + 1) if converted[0] is None else converted[0][None]
            w_o = convert_out(i + 1) if converted[1] is None else converted[1][None]
        else:
            xs = outs[0]
    return _rmsnorm(xs, final_norm, F32).reshape(batch, seq, d)
```
